```python
import jax, jax.numpy as jnp
from jax import lax
import numpy as np

D_MODEL = 1024
BATCH = 32
SEQ = 2048
DEPTH = 2

PLE_DIM = 256
BLOCK = 128
EPS = 1e-6
NEG = -1e30

SWA_HEADS = 8
SWA_KV_HEADS = 2
SWA_HEAD_DIM = 64
SWA_WINDOW = 128
SWA_WIDTH = SWA_HEADS * SWA_HEAD_DIM
SWA_KV_WIDTH = SWA_KV_HEADS * SWA_HEAD_DIM

MLA_HEADS = 8
MLA_NOPE = 64
MLA_ROPE = 32
MLA_V = 64
MLA_Q_LORA = 256
MLA_KV_LORA = 128
MLA_WIDTH = MLA_HEADS * MLA_V
MLA_QK = MLA_NOPE + MLA_ROPE
ROPE_THETA = 10000.0

IN_SIZES = (SWA_WIDTH, SWA_KV_WIDTH, SWA_KV_WIDTH, SWA_WIDTH,
            MLA_Q_LORA, MLA_KV_LORA, MLA_ROPE, MLA_WIDTH,
            D_MODEL, D_MODEL)
IN_WIDTH = sum(IN_SIZES)

kernel_name = "hybrid_swa_sink_mla_gated_merge"


def rms_norm(x, g):
    xf = x.astype(jnp.float32)
    y = xf * lax.rsqrt(jnp.mean(xf * xf, axis=-1, keepdims=True) + EPS)
    return (y * g.astype(jnp.float32)).astype(x.dtype)


def split_columns(z, sizes):
    idx = []
    acc = 0
    for sz in sizes[:-1]:
        acc += sz
        idx.append(acc)
    return jnp.split(z, idx, axis=-1)


def alibi_slopes(n):
    return jnp.exp2(-8.0 * (jnp.arange(n, dtype=jnp.float32) + 1.0) / n)


def apply_rope(x, pos):
    r = x.shape[-1]
    inv = ROPE_THETA ** (-jnp.arange(0, r, 2, dtype=jnp.float32) / r)
    ang = pos.astype(jnp.float32)[..., None] * inv
    cos = jnp.cos(ang)[:, :, None, :]
    sin = jnp.sin(ang)[:, :, None, :]
    xf = x.astype(jnp.float32)
    x1, x2 = xf[..., : r // 2], xf[..., r // 2:]
    out = jnp.concatenate([x1 * cos - x2 * sin, x2 * cos + x1 * sin], axis=-1)
    return out.astype(x.dtype)


def swa_sink_attention(q, k, v, sink, pos):
    b, s, h, dh = q.shape
    kvh = k.shape[2]
    g = h // kvh
    nb = s // BLOCK
    qb = q.reshape(b, nb, BLOCK, kvh, g, dh)

    def band(t):
        tail = t.shape[2:]
        pad = jnp.zeros((b, BLOCK) + tail, t.dtype)
        prev = jnp.concatenate([pad, t[:, :-BLOCK]], axis=1).reshape((b, nb, BLOCK) + tail)
        cur = t.reshape((b, nb, BLOCK) + tail)
        return jnp.concatenate([prev, cur], axis=2)

    kb, vb, pk = band(k), band(v), band(pos)
    pq = pos.reshape(b, nb, BLOCK)
    scores = jnp.einsum('bnqkgd,bnskd->bnkgqs', qb, kb,
                        preferred_element_type=jnp.float32) * (dh ** -0.5)
    dist = (pq[:, :, :, None] - pk[:, :, None, :]).astype(jnp.float32)
    slopes = alibi_slopes(h).reshape(kvh, g)
    scores = scores - slopes[None, None, :, :, None, None] * dist[:, :, None, None, :, :]
    n_i = jnp.arange(nb)[:, None, None]
    q_i = jnp.arange(BLOCK)[None, :, None]
    k_j = jnp.arange(2 * BLOCK)[None, None, :]
    t_abs = n_i * BLOCK + q_i
    s_abs = n_i * BLOCK - BLOCK + k_j
    valid = (s_abs >= 0) & (s_abs <= t_abs) & (t_abs - s_abs < SWA_WINDOW)
    scores = jnp.where(valid[None, :, None, None, :, :], scores, NEG)
    sink_b = sink.astype(jnp.float32).reshape(kvh, g)[None, None, :, :, None]
    m = jnp.maximum(jnp.max(scores, axis=-1), sink_b)
    e = jnp.exp(scores - m[..., None])
    denom = jnp.sum(e, axis=-1) + jnp.exp(sink_b - m)
    probs = e / denom[..., None]
    out = jnp.einsum('bnkgqs,bnskd->bnqkgd', probs.astype(v.dtype), vb)
    return out.reshape(b, s, h * dh)


def mla_causal_attention(q, k, v):
    b, s, h, dq = q.shape
    nb = s // BLOCK
    qb = q.reshape(b, nb, BLOCK, h, dq).transpose(1, 0, 2, 3, 4)
    kpos = jnp.arange(s)
    scale = dq ** -0.5

    def one_block(args):
        qblk, n = args
        sc = jnp.einsum('bqhd,bshd->bhqs', qblk, k,
                        preferred_element_type=jnp.float32) * scale
        qpos = n * BLOCK + jnp.arange(BLOCK)
        sc = jnp.where(kpos[None, :] <= qpos[:, None], sc, NEG)
        pr = jax.nn.softmax(sc, axis=-1)
        return jnp.einsum('bhqs,bshd->bqhd', pr.astype(v.dtype), v)

    out = lax.map(one_block, (qb, jnp.arange(nb)))
    return out.transpose(1, 0, 2, 3, 4).reshape(b, s, h * v.shape[-1])


def setup_inputs(seed: int = 0) -> dict:
    key = jax.random.key(seed)
    ks = jax.random.split(key, 20)
    f32 = jnp.float32

    def nrm(k, shape, fan_in):
        return jax.random.normal(k, shape, f32) * (fan_in ** -0.5)

    def gain(k, shape):
        return 1.0 + 0.02 * jax.random.normal(k, shape, f32)

    x = jax.random.normal(ks[0], (BATCH, SEQ, D_MODEL), f32)
    p = jax.random.normal(ks[1], (DEPTH, BATCH, SEQ, PLE_DIM), f32)
    positions = jnp.broadcast_to(jnp.arange(SEQ, dtype=jnp.int32)[None, :], (BATCH, SEQ))
    return {
        "x": x,
        "p": p,
        "positions": positions,
        "g_mix": gain(ks[2], (DEPTH, D_MODEL)),
        "w_in": nrm(ks[3], (DEPTH, D_MODEL, IN_WIDTH), D_MODEL),
        "sink": 0.5 * jax.random.normal(ks[4], (DEPTH, SWA_HEADS), f32),
        "g_q": gain(ks[5], (DEPTH, MLA_Q_LORA)),
        "w_uq": nrm(ks[6], (DEPTH, MLA_Q_LORA, MLA_HEADS * MLA_QK), MLA_Q_LORA),
        "g_kv": gain(ks[7], (DEPTH, MLA_KV_LORA)),
        "w_ukv": nrm(ks[8], (DEPTH, MLA_KV_LORA, MLA_HEADS * (MLA_NOPE + MLA_V)), MLA_KV_LORA),
        "w_br_a": nrm(ks[9], (DEPTH, SWA_WIDTH, D_MODEL), SWA_WIDTH),
        "w_br_b": nrm(ks[10], (DEPTH, MLA_WIDTH, D_MODEL), MLA_WIDTH),
        "w_out": nrm(ks[11], (DEPTH, D_MODEL, D_MODEL), D_MODEL),
        "g_ple": gain(ks[12], (DEPTH, D_MODEL)),
        "w_ple_gate": nrm(ks[13], (DEPTH, D_MODEL, D_MODEL), D_MODEL),
        "w_ple_proj": nrm(ks[14], (DEPTH, PLE_DIM, D_MODEL), PLE_DIM),
        "g_final": gain(ks[15], (D_MODEL,)),
    }


def reference(x, p, positions, g_mix, w_in, sink, g_q, w_uq, g_kv, w_ukv,
              w_br_a, w_br_b, w_out, g_ple, w_ple_gate, w_ple_proj, g_final):
    b, s, _ = x.shape
    for i in range(DEPTH):
        h = rms_norm(x, g_mix[i])
        z = h @ w_in[i]
        (a_q, a_k, a_v, a_gate, b_qd, b_kvd, b_kr, b_gate,
         m_a, m_b) = split_columns(z, IN_SIZES)

        qa = a_q.reshape(b, s, SWA_HEADS, SWA_HEAD_DIM)
        ka = a_k.reshape(b, s, SWA_KV_HEADS, SWA_HEAD_DIM)
        va = a_v.reshape(b, s, SWA_KV_HEADS, SWA_HEAD_DIM)
        o_a = swa_sink_attention(qa, ka, va, sink[i], positions) * jax.nn.silu(a_gate)

        qb = (rms_norm(b_qd, g_q[i]) @ w_uq[i]).reshape(b, s, MLA_HEADS, MLA_QK)
        q_nope, q_rope = qb[..., :MLA_NOPE], qb[..., MLA_NOPE:]
        q_rope = apply_rope(q_rope, positions)
        kv = (rms_norm(b_kvd, g_kv[i]) @ w_ukv[i]).reshape(b, s, MLA_HEADS, MLA_NOPE + MLA_V)
        k_nope, vb = kv[..., :MLA_NOPE], kv[..., MLA_NOPE:]
        k_rope = apply_rope(b_kr[:, :, None, :], positions)
        q_full = jnp.concatenate([q_nope, q_rope], axis=-1)
        k_full = jnp.concatenate(
            [k_nope, jnp.broadcast_to(k_rope, (b, s, MLA_HEADS, MLA_ROPE))], axis=-1)
        o_b = mla_causal_attention(q_full, k_full, vb) * jax.nn.silu(b_gate)

        y = jax.nn.sigmoid(m_a) * (o_a @ w_br_a[i]) + jax.nn.sigmoid(m_b) * (o_b @ w_br_b[i])
        x = x + y @ w_out[i]

        pg = jax.nn.sigmoid(rms_norm(x, g_ple[i]) @ w_ple_gate[i])
        x = x + pg * (p[i].astype(x.dtype) @ w_ple_proj[i])
    return rms_norm(x, g_final)
```

```python
import functools

import jax
import jax.numpy as jnp
from jax import lax
from jax.experimental import pallas as pl
from jax.experimental.pallas import tpu as pltpu

F32 = jnp.float32
BF16 = jnp.bfloat16

D_MODEL = 1024
PLE_DIM = 256
BLOCK = 128
LANES = 128
EPS = 1e-6
NEG = -1e30

SWA_HEADS = 8
SWA_KV_HEADS = 2
SWA_HEAD_DIM = 64
SWA_WIDTH = SWA_HEADS * SWA_HEAD_DIM
SWA_KV_WIDTH = SWA_KV_HEADS * SWA_HEAD_DIM

MLA_HEADS = 8
MLA_NOPE = 64
MLA_ROPE = 32
MLA_V = 64
MLA_Q_LORA = 256
MLA_KV_LORA = 128
MLA_WIDTH = MLA_HEADS * MLA_V
MLA_QK = MLA_NOPE + MLA_ROPE
MLA_PAD = MLA_HEADS * LANES
ROPE_THETA = 10000.0

_C_AQ = 0
_C_AK = _C_AQ + SWA_WIDTH
_C_AV = _C_AK + SWA_KV_WIDTH
_C_AG = _C_AV + SWA_KV_WIDTH
_C_QD = _C_AG + SWA_WIDTH
_C_KVD = _C_QD + MLA_Q_LORA
_C_KR = _C_KVD + MLA_KV_LORA
_C_BG = _C_KR + LANES
_C_MA = _C_BG + MLA_WIDTH
_C_MB = _C_MA + D_MODEL
_C_END = _C_MB + D_MODEL

_DIST_MASK = 2.0 ** 8 * 1e30

VMEM_LIMIT = 56 * 1024 * 1024


def _rms(x, g):
    return x * lax.rsqrt(jnp.mean(x * x, axis=-1, keepdims=True) + EPS) * g


def _sigmoid(x):
    return 1.0 / (1.0 + jnp.exp(-x))


def _col_bcast(row):
    return jnp.broadcast_to(row, (LANES, LANES)).T


def _dot(a, b):
    return jnp.dot(a, b, preferred_element_type=F32)


def _dot_nt(a, b):
    return lax.dot_general(a, b, (((1,), (1,)), ((), ())), preferred_element_type=F32)


def _in_proj_kernel(x_ref, pos_ref, inv_ref, gmix_ref, win_ref, gq_ref, wuq_ref, gkv_ref,
                    wuk_ref, wuv_ref,
                    aq_ref, ak_ref, av_ref, ag_ref, bg_ref, ma_ref, mb_ref, q_ref, k_ref, v_ref,
                    *, tm):
    h = _rms(x_ref[...], gmix_ref[...]).astype(BF16)

    def proj(lo, hi):
        return _dot(h, win_ref[:, lo:hi])

    aq_ref[...] = (proj(_C_AQ, _C_AK) * (SWA_HEAD_DIM ** -0.5)).astype(BF16)
    ak_ref[...] = proj(_C_AK, _C_AV).astype(BF16)
    av_ref[...] = proj(_C_AV, _C_AG).astype(BF16)
    ag_ref[...] = proj(_C_AG, _C_QD).astype(BF16)
    bg_ref[...] = proj(_C_BG, _C_MA).astype(BF16)
    ma_ref[...] = proj(_C_MA, _C_MB).astype(BF16)
    mb_ref[...] = proj(_C_MB, _C_END).astype(BF16)

    pos = pos_ref[0].astype(F32)
    inv = inv_ref[...]
    ang = jnp.concatenate(
        [_col_bcast(pos[g:g + 1, :]) * inv for g in range(tm // LANES)], axis=0)
    cos = jnp.cos(ang)
    sin = jnp.sin(ang)
    lane = lax.broadcasted_iota(jnp.int32, (tm, LANES), 1)
    half = MLA_ROPE // 2
    s_lo = jnp.where((lane >= MLA_NOPE) & (lane < MLA_NOPE + half), -sin, 0.0)
    s_hi = jnp.where((lane >= MLA_NOPE + half) & (lane < MLA_QK), sin, 0.0)

    def rope(t, c, s1, s2):
        return (t * c + pltpu.roll(t, LANES - half, 1) * s1 + pltpu.roll(t, half, 1) * s2)

    qscale = MLA_QK ** -0.5
    qn = _rms(proj(_C_QD, _C_KVD), gq_ref[...]).astype(BF16)
    cq, s1q, s2q = cos * qscale, s_lo * qscale, s_hi * qscale
    for hd in range(MLA_HEADS):
        t = _dot(qn, wuq_ref[:, hd * LANES:(hd + 1) * LANES])
        q_ref[:, hd * LANES:(hd + 1) * LANES] = rope(t, cq, s1q, s2q).astype(BF16)

    kvn = _rms(proj(_C_KVD, _C_KR), gkv_ref[...]).astype(BF16)
    kr = rope(proj(_C_KR, _C_BG), cos, s_lo, s_hi)
    for hd in range(MLA_HEADS):
        t = _dot(kvn, wuk_ref[:, hd * LANES:(hd + 1) * LANES])
        k_ref[:, hd * LANES:(hd + 1) * LANES] = (t + kr).astype(BF16)
    v_ref[...] = _dot(kvn, wuv_ref[...]).astype(BF16)


def _in_proj(x2d, pos3, inv_tile, gmix, win, gq, wuq, gkv, wuk, wuv, *, tm):
    n = x2d.shape[0]
    grid = (n // tm,)
    row = lambda w: pl.BlockSpec((tm, w), lambda i: (i, 0))
    full = lambda a: pl.BlockSpec(a.shape, lambda i: (0,) * a.ndim)
    widths = (SWA_WIDTH, SWA_KV_WIDTH, SWA_KV_WIDTH, SWA_WIDTH, MLA_WIDTH, D_MODEL, D_MODEL,
              MLA_PAD, MLA_PAD, MLA_WIDTH)
    return pl.pallas_call(
        functools.partial(_in_proj_kernel, tm=tm),
        grid=grid,
        in_specs=[row(D_MODEL),
                  pl.BlockSpec((1, tm // LANES, LANES), lambda i: (i, 0, 0)),
                  full(inv_tile), full(gmix), full(win), full(gq), full(wuq), full(gkv),
                  full(wuk), full(wuv)],
        out_specs=[row(w) for w in widths],
        out_shape=[jax.ShapeDtypeStruct((n, w), BF16) for w in widths],
        compiler_params=pltpu.CompilerParams(dimension_semantics=("arbitrary",),
                                             vmem_limit_bytes=VMEM_LIMIT),
        name="in_proj",
    )(x2d, pos3, inv_tile, gmix, win, gq, wuq, gkv, wuk, wuv)


def _swa_kernel(sink_ref, q_ref, k_ref, kh_ref, v_ref, vh_ref, g_ref, pos_ref, posh_ref, o_ref,
                *, tile):
    i = pl.program_id(1)
    nsub = tile // BLOCK
    group = SWA_HEADS // SWA_KV_HEADS

    def variants(main_ref, halo_ref):
        t = jnp.concatenate([halo_ref[0], main_ref[0]], axis=0).astype(F32)
        r = pltpu.roll(t, SWA_HEAD_DIM, 1)
        lo = lax.broadcasted_iota(jnp.int32, t.shape, 1) < SWA_HEAD_DIM
        z = jnp.zeros_like(t)
        return [[jnp.where(lo, t, z).astype(BF16), jnp.where(lo, z, r).astype(BF16)],
                [jnp.where(lo, r, z).astype(BF16), jnp.where(lo, z, t).astype(BF16)]]

    kv = variants(k_ref, kh_ref)
    vv = variants(v_ref, vh_ref)
    pos_rows = jnp.concatenate([posh_ref[0, 0], pos_ref[0, 0]], axis=0)

    ii = lax.broadcasted_iota(jnp.int32, (BLOCK, 2 * BLOCK), 0)
    jj = lax.broadcasted_iota(jnp.int32, (BLOCK, 2 * BLOCK), 1)
    band = (jj > ii) & (jj <= ii + BLOCK)

    for sub in range(nsub):
        rows = slice(sub * BLOCK, (sub + 1) * BLOCK)
        krows = slice(sub * BLOCK, (sub + 2) * BLOCK)
        pq = _col_bcast(pos_rows[sub + 1:sub + 2, :])[:, 0:1]
        pk = jnp.concatenate([pos_rows[sub:sub + 1, :], pos_rows[sub + 1:sub + 2, :]], axis=1)
        dist = (pq - pk).astype(F32)
        valid = band
        if sub == 0:
            valid = band & ((jj >= BLOCK) | (i > 0))
        distm = jnp.where(valid, dist, _DIST_MASK)
        for t in range(SWA_HEADS // 2):
            g = (2 * t) // group
            qp = q_ref[0, rows, t * LANES:(t + 1) * LANES]
            acc = None
            for hf in range(2):
                hd = 2 * t + hf
                slope = 2.0 ** (-8.0 * (hd + 1) / SWA_HEADS)
                sink = sink_ref[hd]
                s = _dot_nt(qp, kv[g][hf][krows]) - slope * distm
                m = jnp.maximum(jnp.max(s, axis=-1, keepdims=True), sink)
                e = jnp.exp(s - m)
                den = jnp.sum(e, axis=-1, keepdims=True) + jnp.exp(sink - m)
                pv = _dot(e.astype(BF16), vv[g][hf][krows]) / den
                acc = pv if acc is None else acc + pv
            gate = g_ref[0, rows, t * LANES:(t + 1) * LANES].astype(F32)
            o_ref[0, rows, t * LANES:(t + 1) * LANES] = (acc * gate * _sigmoid(gate)).astype(BF16)


def _swa(sink, q, k, v, gate, pos4, posh, *, tile):
    b, s, _ = q.shape
    nsub = tile // BLOCK
    grid = (b, s // tile)
    main = lambda w: pl.BlockSpec((1, tile, w), lambda bi, i: (bi, i, 0))
    halo = pl.BlockSpec((1, BLOCK, SWA_KV_WIDTH),
                        lambda bi, i: (bi, jnp.maximum(i * nsub - 1, 0), 0))
    return pl.pallas_call(
        functools.partial(_swa_kernel, tile=tile),
        grid=grid,
        in_specs=[pl.BlockSpec(memory_space=pltpu.SMEM),
                  main(SWA_WIDTH), main(SWA_KV_WIDTH), halo, main(SWA_KV_WIDTH), halo,
                  main(SWA_WIDTH),
                  pl.BlockSpec((1, 1, nsub, LANES), lambda bi, i: (bi, i, 0, 0)),
                  pl.BlockSpec((1, 1, 1, LANES),
                               lambda bi, i: (bi, jnp.maximum(i * nsub - 1, 0), 0, 0))],
        out_specs=main(SWA_WIDTH),
        out_shape=jax.ShapeDtypeStruct((b, s, SWA_WIDTH), BF16),
        compiler_params=pltpu.CompilerParams(dimension_semantics=("arbitrary", "arbitrary"),
                                             vmem_limit_bytes=VMEM_LIMIT),
        name="swa",
    )(sink, q, k, k, v, v, gate, pos4, posh)


def _mla_kernel(q_ref, k_ref, v_ref, g_ref, o_ref, *, tq):
    qi = pl.program_id(1)
    row = lax.broadcasted_iota(jnp.int32, (tq, tq), 0)
    col = lax.broadcasted_iota(jnp.int32, (tq, tq), 1)
    causal = col <= row
    lo = lax.broadcasted_iota(jnp.int32, (tq, LANES), 1) < MLA_V

    for pair in range(MLA_HEADS // 2):
        vcols = slice(pair * LANES, (pair + 1) * LANES)
        res = []
        for hf in range(2):
            hcols = slice((2 * pair + hf) * LANES, (2 * pair + hf + 1) * LANES)
            q = q_ref[0, :, hcols]

            def step(j, carry, masked, q=q, hcols=hcols, vcols=vcols):
                m, l, acc = carry
                start = pl.multiple_of(j * tq, tq)
                k = k_ref[0, pl.ds(start, tq), hcols]
                v = v_ref[0, pl.ds(start, tq), vcols]
                s = _dot_nt(q, k)
                if masked:
                    s = jnp.where(causal, s, NEG)
                m_new = jnp.maximum(m, jnp.max(s, axis=-1, keepdims=True))
                alpha = jnp.exp(m - m_new)
                p = jnp.exp(s - m_new)
                l = alpha * l + jnp.sum(p, axis=-1, keepdims=True)
                acc = alpha * acc + _dot(p.astype(BF16), v)
                return m_new, l, acc

            init = (jnp.full((tq, 1), NEG, F32), jnp.zeros((tq, 1), F32),
                    jnp.zeros((tq, LANES), F32))
            carry = lax.fori_loop(0, qi, functools.partial(step, masked=False), init)
            _, l, acc = step(qi, carry, True)
            res.append(acc / l)
        gate = g_ref[0, :, vcols].astype(F32)
        o = jnp.where(lo, res[0], res[1])
        o_ref[0, :, vcols] = (o * gate * _sigmoid(gate)).astype(BF16)


def _mla(q, k, v, gate, *, tq):
    b, s, _ = q.shape
    grid = (b, s // tq)
    blk = lambda w: pl.BlockSpec((1, tq, w), lambda bi, i: (bi, i, 0))
    seq = lambda w: pl.BlockSpec((1, s, w), lambda bi, i: (bi, 0, 0))
    return pl.pallas_call(
        functools.partial(_mla_kernel, tq=tq),
        grid=grid,
        in_specs=[blk(MLA_PAD), seq(MLA_PAD), seq(MLA_WIDTH), blk(MLA_WIDTH)],
        out_specs=blk(MLA_WIDTH),
        out_shape=jax.ShapeDtypeStruct((b, s, MLA_WIDTH), BF16),
        compiler_params=pltpu.CompilerParams(dimension_semantics=("arbitrary", "arbitrary"),
                                             vmem_limit_bytes=VMEM_LIMIT),
        name="mla",
    )(q, k, v, gate)


def _merge_kernel(oa_ref, ob_ref, ma_ref, mb_ref, x_ref, p_ref, wa_ref, wb_ref, wout_ref,
                  gple_ref, wpg_ref, wpp_ref, gfin_ref, out_ref, *, final):
    ya = _dot(oa_ref[...], wa_ref[...])
    yb = _dot(ob_ref[...], wb_ref[...])
    y = _sigmoid(ma_ref[...].astype(F32)) * ya + _sigmoid(mb_ref[...].astype(F32)) * yb
    x1 = x_ref[...] + _dot(y.astype(BF16), wout_ref[...])
    hn = _rms(x1, gple_ref[...]).astype(BF16)
    pg = _sigmoid(_dot(hn, wpg_ref[...]))
    pp = _dot(p_ref[...].astype(BF16), wpp_ref[...])
    x2 = x1 + pg * pp
    if final:
        x2 = _rms(x2, gfin_ref[...])
    out_ref[...] = x2


def _merge(oa, ob, ma, mb, x2d, p2d, wa, wb, wout, gple, wpg, wpp, gfin, *, tm, final):
    n = x2d.shape[0]
    row = lambda w: pl.BlockSpec((tm, w), lambda i: (i, 0))
    full = lambda a: pl.BlockSpec(a.shape, lambda i: (0,) * a.ndim)
    return pl.pallas_call(
        functools.partial(_merge_kernel, final=final),
        grid=(n // tm,),
        in_specs=[row(SWA_WIDTH), row(MLA_WIDTH), row(D_MODEL), row(D_MODEL), row(D_MODEL),
                  row(PLE_DIM), full(wa), full(wb), full(wout), full(gple), full(wpg),
                  full(wpp), full(gfin)],
        out_specs=row(D_MODEL),
        out_shape=jax.ShapeDtypeStruct((n, D_MODEL), F32),
        compiler_params=pltpu.CompilerParams(dimension_semantics=("arbitrary",),
                                             vmem_limit_bytes=VMEM_LIMIT),
        name="merge",
    )(oa, ob, ma, mb, x2d, p2d, wa, wb, wout, gple, wpg, wpp, gfin)


def _pack_weights(w_in, w_uq, w_ukv):
    d = w_in.shape[0]
    kr0 = _C_KR
    kr = w_in[:, :, kr0:kr0 + MLA_ROPE]
    kr_pad = jnp.pad(kr, ((0, 0), (0, 0), (MLA_NOPE, LANES - MLA_QK)))
    win = jnp.concatenate([w_in[:, :, :kr0], kr_pad, w_in[:, :, kr0 + MLA_ROPE:]], axis=-1)
    wuq = w_uq.reshape(d, MLA_Q_LORA, MLA_HEADS, MLA_QK)
    wuq = jnp.pad(wuq, ((0, 0), (0, 0), (0, 0), (0, LANES - MLA_QK))).reshape(d, MLA_Q_LORA, MLA_PAD)
    wukv = w_ukv.reshape(d, MLA_KV_LORA, MLA_HEADS, MLA_NOPE + MLA_V)
    wuk = jnp.pad(wukv[..., :MLA_NOPE], ((0, 0), (0, 0), (0, 0), (0, LANES - MLA_NOPE)))
    wuk = wuk.reshape(d, MLA_KV_LORA, MLA_PAD)
    wuv = wukv[..., MLA_NOPE:].reshape(d, MLA_KV_LORA, MLA_WIDTH)
    return win.astype(BF16), wuq.astype(BF16), wuk.astype(BF16), wuv.astype(BF16)


def _rope_inv_tile():
    inv = ROPE_THETA ** (-jnp.arange(0, MLA_ROPE, 2, dtype=F32) / MLA_ROPE)
    z = jnp.zeros((MLA_NOPE,), F32)
    return jnp.concatenate([z, inv, inv, jnp.zeros((LANES - MLA_QK,), F32)])[None, :]


def kernel(x, p, positions, g_mix, w_in, sink, g_q, w_uq, g_kv, w_ukv, w_br_a, w_br_b, w_out,
           g_ple, w_ple_gate, w_ple_proj, g_final):
    b, s, _ = x.shape
    depth = w_in.shape[0]
    n = b * s
    tm_in, tm_merge, swa_tile, tq = 256, 256, 512, 128
    assert n % tm_in == 0 and n % tm_merge == 0 and s % swa_tile == 0 and s % tq == 0

    win, wuq, wuk, wuv = _pack_weights(w_in, w_uq, w_ukv)
    wa, wb, wout = w_br_a.astype(BF16), w_br_b.astype(BF16), w_out.astype(BF16)
    wpg, wpp = w_ple_gate.astype(BF16), w_ple_proj.astype(BF16)
    inv_tile = _rope_inv_tile()
    pos_in = positions.reshape(n // tm_in, tm_in // LANES, LANES)
    pos4 = positions.reshape(b, s // swa_tile, swa_tile // BLOCK, LANES)
    posh = positions.reshape(b, s // BLOCK, 1, LANES)
    gfin = g_final[None, :]

    x2d = x.reshape(n, D_MODEL)
    for i in range(depth):
        aq, ak, av, ag, bg, ma, mb, q, k, v = _in_proj(
            x2d, pos_in, inv_tile, g_mix[i][None, :], win[i], g_q[i][None, :], wuq[i],
            g_kv[i][None, :], wuk[i], wuv[i], tm=tm_in)
        r3 = lambda t: t.reshape(b, s, t.shape[-1])
        oa = _swa(sink[i], r3(aq), r3(ak), r3(av), r3(ag), pos4, posh, tile=swa_tile)
        ob = _mla(r3(q), r3(k), r3(v), r3(bg), tq=tq)
        x2d = _merge(oa.reshape(n, SWA_WIDTH), ob.reshape(n, MLA_WIDTH), ma, mb, x2d,
                     p[i].reshape(n, PLE_DIM), wa[i], wb[i], wout[i], g_ple[i][None, :], wpg[i],
                     wpp[i], gfin, tm=tm_merge, final=(i == depth - 1))
    return x2d.reshape(b, s, D_MODEL)
```

```python
import functools

import jax
import jax.numpy as jnp
from jax import lax
from jax.experimental import pallas as pl
from jax.experimental.pallas import tpu as pltpu

F32 = jnp.float32
BF16 = jnp.bfloat16

D_MODEL = 1024
PLE_DIM = 256
BLOCK = 128
LANES = 128
EPS = 1e-6
NEG = -1e30
LOG2E = 1.4426950408889634

SWA_HEADS = 8
SWA_KV_HEADS = 2
SWA_HEAD_DIM = 64
SWA_WIDTH = SWA_HEADS * SWA_HEAD_DIM
SWA_KV_WIDTH = SWA_KV_HEADS * SWA_HEAD_DIM

MLA_HEADS = 8
MLA_NOPE = 64
MLA_ROPE = 32
MLA_V = 64
MLA_Q_LORA = 256
MLA_KV_LORA = 128
MLA_WIDTH = MLA_HEADS * MLA_V
MLA_QK = MLA_NOPE + MLA_ROPE
MLA_PAD = MLA_HEADS * LANES
ROPE_THETA = 10000.0

_C_AQ = 0
_C_AK = _C_AQ + SWA_WIDTH
_C_AV = _C_AK + SWA_KV_WIDTH
_C_AG = _C_AV + SWA_KV_WIDTH
_C_QD = _C_AG + SWA_WIDTH
_C_KVD = _C_QD + MLA_Q_LORA
_C_KR = _C_KVD + MLA_KV_LORA
_C_BG = _C_KR + LANES
_C_MA = _C_BG + MLA_WIDTH
_C_MB = _C_MA + D_MODEL
_C_END = _C_MB + D_MODEL

_DIST_MASK = 2.0 ** 8 * 1e30

VMEM_LIMIT = 56 * 1024 * 1024


def _rms(x, g):
    return x * lax.rsqrt(jnp.mean(x * x, axis=-1, keepdims=True) + EPS) * g


def _sigmoid(x):
    return 1.0 / (1.0 + jnp.exp(-x))


def _col_bcast(row):
    return jnp.broadcast_to(row, (LANES, LANES)).T


def _dot(a, b):
    return jnp.dot(a, b, preferred_element_type=F32)


def _dot_nt(a, b):
    return lax.dot_general(a, b, (((1,), (1,)), ((), ())), preferred_element_type=F32)


def _in_proj_kernel(x_ref, pos_ref, inv_ref, gmix_ref, win_ref, gq_ref, wuq_ref, gkv_ref,
                    wuk_ref, wuv_ref,
                    aq_ref, ak_ref, av_ref, ag_ref, bg_ref, ma_ref, mb_ref, q_ref, k_ref, v_ref,
                    *, tm):
    h = _rms(x_ref[...], gmix_ref[...]).astype(BF16)

    def proj(lo, hi):
        return _dot(h, win_ref[:, lo:hi])

    aq_ref[...] = (proj(_C_AQ, _C_AK) * (SWA_HEAD_DIM ** -0.5)).astype(BF16)
    ak_ref[...] = proj(_C_AK, _C_AV).astype(BF16)
    av_ref[...] = proj(_C_AV, _C_AG).astype(BF16)
    ag_ref[...] = proj(_C_AG, _C_QD).astype(BF16)
    bg_ref[...] = proj(_C_BG, _C_MA).astype(BF16)
    ma_ref[...] = proj(_C_MA, _C_MB).astype(BF16)
    mb_ref[...] = proj(_C_MB, _C_END).astype(BF16)

    pos = pos_ref[0].astype(F32)
    inv = inv_ref[...]
    ang = jnp.concatenate(
        [_col_bcast(pos[g:g + 1, :]) * inv for g in range(tm // LANES)], axis=0)
    cos = jnp.cos(ang)
    sin = jnp.sin(ang)
    lane = lax.broadcasted_iota(jnp.int32, (tm, LANES), 1)
    half = MLA_ROPE // 2
    s_lo = jnp.where((lane >= MLA_NOPE) & (lane < MLA_NOPE + half), -sin, 0.0)
    s_hi = jnp.where((lane >= MLA_NOPE + half) & (lane < MLA_QK), sin, 0.0)

    def rope(t, c, s1, s2):
        return (t * c + pltpu.roll(t, LANES - half, 1) * s1 + pltpu.roll(t, half, 1) * s2)

    qscale = MLA_QK ** -0.5 * LOG2E
    qn =_rms(proj(_C_QD, _C_KVD), gq_ref[...]).astype(BF16)
    cq, s1q, s2q = cos * qscale, s_lo * qscale, s_hi * qscale
    for hd in range(MLA_HEADS):
        t = _dot(qn, wuq_ref[:, hd * LANES:(hd + 1) * LANES])
        q_ref[:, hd * LANES:(hd + 1) * LANES] = rope(t, cq, s1q, s2q).astype(BF16)

    kvn = _rms(proj(_C_KVD, _C_KR), gkv_ref[...]).astype(BF16)
    kr = rope(proj(_C_KR, _C_BG), cos, s_lo, s_hi)
    for hd in range(MLA_HEADS):
        t = _dot(kvn, wuk_ref[:, hd * LANES:(hd + 1) * LANES])
        k_ref[:, hd * LANES:(hd + 1) * LANES] = (t + kr).astype(BF16)
    for hd in range(MLA_HEADS):
        ones = jnp.where((lane < MLA_V) == (hd % 2 == 1), 1.0, 0.0)
        t = _dot(kvn, wuv_ref[:, hd * LANES:(hd + 1) * LANES])
        v_ref[:, hd * LANES:(hd + 1) * LANES] = (t + ones).astype(BF16)


def _in_proj(x2d, pos3, inv_tile, gmix, win, gq, wuq, gkv, wuk, wuv, *, tm):
    n = x2d.shape[0]
    grid = (n // tm,)
    row = lambda w: pl.BlockSpec((tm, w), lambda i: (i, 0))
    full = lambda a: pl.BlockSpec(a.shape, lambda i: (0,) * a.ndim)
    widths = (SWA_WIDTH, SWA_KV_WIDTH, SWA_KV_WIDTH, SWA_WIDTH, MLA_WIDTH, D_MODEL, D_MODEL,
              MLA_PAD, MLA_PAD, MLA_PAD)
    return pl.pallas_call(
        functools.partial(_in_proj_kernel, tm=tm),
        grid=grid,
        in_specs=[row(D_MODEL),
                  pl.BlockSpec((1, tm // LANES, LANES), lambda i: (i, 0, 0)),
                  full(inv_tile), full(gmix), full(win), full(gq), full(wuq), full(gkv),
                  full(wuk), full(wuv)],
        out_specs=[row(w) for w in widths],
        out_shape=[jax.ShapeDtypeStruct((n, w), BF16) for w in widths],
        compiler_params=pltpu.CompilerParams(dimension_semantics=("arbitrary",),
                                             vmem_limit_bytes=VMEM_LIMIT),
        name="in_proj",
    )(x2d, pos3, inv_tile, gmix, win, gq, wuq, gkv, wuk, wuv)


def _swa_kernel(sink_ref, q_ref, k_ref, kh_ref, v_ref, vh_ref, g_ref, pos_ref, posh_ref, o_ref,
                *, tile):
    i = pl.program_id(1)
    nsub = tile // BLOCK
    group = SWA_HEADS // SWA_KV_HEADS

    def variants(main_ref, halo_ref):
        t = jnp.concatenate([halo_ref[0], main_ref[0]], axis=0).astype(F32)
        r = pltpu.roll(t, SWA_HEAD_DIM, 1)
        lo = lax.broadcasted_iota(jnp.int32, t.shape, 1) < SWA_HEAD_DIM
        z = jnp.zeros_like(t)
        return [[jnp.where(lo, t, z).astype(BF16), jnp.where(lo, z, r).astype(BF16)],
                [jnp.where(lo, r, z).astype(BF16), jnp.where(lo, z, t).astype(BF16)]]

    kv = variants(k_ref, kh_ref)
    vv = variants(v_ref, vh_ref)
    pos_rows = jnp.concatenate([posh_ref[0, 0], pos_ref[0, 0]], axis=0)

    ii = lax.broadcasted_iota(jnp.int32, (BLOCK, 2 * BLOCK), 0)
    jj = lax.broadcasted_iota(jnp.int32, (BLOCK, 2 * BLOCK), 1)
    band = (jj > ii) & (jj <= ii + BLOCK)

    for sub in range(nsub):
        rows = slice(sub * BLOCK, (sub + 1) * BLOCK)
        krows = slice(sub * BLOCK, (sub + 2) * BLOCK)
        pq = _col_bcast(pos_rows[sub + 1:sub + 2, :])[:, 0:1]
        pk = jnp.concatenate([pos_rows[sub:sub + 1, :], pos_rows[sub + 1:sub + 2, :]], axis=1)
        dist = (pq - pk).astype(F32)
        valid = band
        if sub == 0:
            valid = band & ((jj >= BLOCK) | (i > 0))
        distm = jnp.where(valid, dist, _DIST_MASK)
        for t in range(SWA_HEADS // 2):
            g = (2 * t) // group
            qp = q_ref[0, rows, t * LANES:(t + 1) * LANES]
            acc = None
            for hf in range(2):
                hd = 2 * t + hf
                slope = 2.0 ** (-8.0 * (hd + 1) / SWA_HEADS)
                sink = sink_ref[hd]
                s = _dot_nt(qp, kv[g][hf][krows]) - slope * distm
                m = jnp.maximum(jnp.max(s, axis=-1, keepdims=True), sink)
                e = jnp.exp(s - m)
                den = jnp.sum(e, axis=-1, keepdims=True) + jnp.exp(sink - m)
                pv = _dot(e.astype(BF16), vv[g][hf][krows]) / den
                acc = pv if acc is None else acc + pv
            gate = g_ref[0, rows, t * LANES:(t + 1) * LANES].astype(F32)
            o_ref[0, rows, t * LANES:(t + 1) * LANES] = (acc * gate * _sigmoid(gate)).astype(BF16)


def _swa(sink, q, k, v, gate, pos4, posh, *, tile):
    b, s, _ = q.shape
    nsub = tile // BLOCK
    grid = (b, s // tile)
    main = lambda w: pl.BlockSpec((1, tile, w), lambda bi, i: (bi, i, 0))
    halo = pl.BlockSpec((1, BLOCK, SWA_KV_WIDTH),
                        lambda bi, i: (bi, jnp.maximum(i * nsub - 1, 0), 0))
    return pl.pallas_call(
        functools.partial(_swa_kernel, tile=tile),
        grid=grid,
        in_specs=[pl.BlockSpec(memory_space=pltpu.SMEM),
                  main(SWA_WIDTH), main(SWA_KV_WIDTH), halo, main(SWA_KV_WIDTH), halo,
                  main(SWA_WIDTH),
                  pl.BlockSpec((1, 1, nsub, LANES), lambda bi, i: (bi, i, 0, 0)),
                  pl.BlockSpec((1, 1, 1, LANES),
                               lambda bi, i: (bi, jnp.maximum(i * nsub - 1, 0), 0, 0))],
        out_specs=main(SWA_WIDTH),
        out_shape=jax.ShapeDtypeStruct((b, s, SWA_WIDTH), BF16),
        compiler_params=pltpu.CompilerParams(dimension_semantics=("arbitrary", "arbitrary"),
                                             vmem_limit_bytes=VMEM_LIMIT),
        name="swa",
    )(sink, q, k, k, v, v, gate, pos4, posh)


def _mla_kernel(q_ref, k_ref, v_ref, g_ref, o_ref, m_sc, acc_sc, *, tq, tk):
    qi = pl.program_id(1)
    row = lax.broadcasted_iota(jnp.int32, (tq, tq), 0)
    col = lax.broadcasted_iota(jnp.int32, (tq, tq), 1)
    causal = col <= row
    m_sc[...] = jnp.full(m_sc.shape, NEG, F32)
    acc_sc[...] = jnp.zeros(acc_sc.shape, F32)

    def chunk(start, width, masked):
        for hd in range(MLA_HEADS):
            hcols = slice(hd * LANES, (hd + 1) * LANES)
            s = _dot_nt(q_ref[0, :, hcols], k_ref[0, pl.ds(start, width), hcols])
            if masked:
                s = jnp.where(causal, s, NEG)
            m_prev = m_sc[hd]
            m_new = jnp.maximum(m_prev, jnp.max(s, axis=-1, keepdims=True))
            alpha = jnp.exp2(m_prev - m_new)
            p = jnp.exp2(s - jnp.tile(m_new, (1, width // LANES)))
            pv = _dot(p.astype(BF16), v_ref[0, pl.ds(start, width), hcols])
            acc_sc[hd] = alpha * acc_sc[hd] + pv
            m_sc[hd] = m_new

    per = tk // tq
    n_wide = qi // per

    def wide_body(j, c):
        chunk(pl.multiple_of(j * tk, tk), tk, False)
        return c

    def narrow_body(j, c):
        chunk(pl.multiple_of(j * tq, tq), tq, False)
        return c

    lax.fori_loop(0, n_wide, wide_body, 0)
    if per > 1:
        lax.fori_loop(n_wide * per, qi, narrow_body, 0)
    chunk(pl.multiple_of(qi * tq, tq), tq, True)

    lo = lax.broadcasted_iota(jnp.int32, (tq, LANES), 1) < MLA_V
    for pair in range(MLA_HEADS // 2):
        acc_e = acc_sc[2 * pair]
        acc_o = acc_sc[2 * pair + 1]
        num = jnp.where(lo, acc_e, acc_o)
        den = pltpu.roll(jnp.where(lo, acc_o, acc_e), MLA_V, 1)
        vcols = slice(pair * LANES, (pair + 1) * LANES)
        gate = g_ref[0, :, vcols].astype(F32)
        o_ref[0, :, vcols] = (num / den * gate * _sigmoid(gate)).astype(BF16)


def _mla(q, k, v, gate, *, tq, tk):
    b, s, _ = q.shape
    grid = (b, s // tq)
    blk = lambda w: pl.BlockSpec((1, tq, w), lambda bi, i: (bi, i, 0))
    seq = lambda w: pl.BlockSpec((1, s, w), lambda bi, i: (bi, 0, 0))
    return pl.pallas_call(
        functools.partial(_mla_kernel, tq=tq, tk=tk),
        grid=grid,
        in_specs=[blk(MLA_PAD), seq(MLA_PAD), seq(MLA_PAD), blk(MLA_WIDTH)],
        out_specs=blk(MLA_WIDTH),
        out_shape=jax.ShapeDtypeStruct((b, s, MLA_WIDTH), BF16),
        scratch_shapes=[pltpu.VMEM((MLA_HEADS, tq, LANES), F32),
                        pltpu.VMEM((MLA_HEADS, tq, LANES), F32)],
        compiler_params=pltpu.CompilerParams(dimension_semantics=("arbitrary", "arbitrary"),
                                             vmem_limit_bytes=VMEM_LIMIT),
        name="mla",
    )(q, k, v, gate)


def _merge_kernel(oa_ref, ob_ref, ma_ref, mb_ref, x_ref, p_ref, wa_ref, wb_ref, wout_ref,
                  gple_ref, wpg_ref, wpp_ref, gfin_ref, out_ref, *, final):
    ya = _dot(oa_ref[...], wa_ref[...])
    yb = _dot(ob_ref[...], wb_ref[...])
    y = _sigmoid(ma_ref[...].astype(F32)) * ya + _sigmoid(mb_ref[...].astype(F32)) * yb
    x1 = x_ref[...] + _dot(y.astype(BF16), wout_ref[...])
    hn = _rms(x1, gple_ref[...]).astype(BF16)
    pg = _sigmoid(_dot(hn, wpg_ref[...]))
    pp = _dot(p_ref[...].astype(BF16), wpp_ref[...])
    x2 = x1 + pg * pp
    if final:
        x2 = _rms(x2, gfin_ref[...])
    out_ref[...] = x2


def _merge(oa, ob, ma, mb, x2d, p2d, wa, wb, wout, gple, wpg, wpp, gfin, *, tm, final):
    n = x2d.shape[0]
    row = lambda w: pl.BlockSpec((tm, w), lambda i: (i, 0))
    full = lambda a: pl.BlockSpec(a.shape, lambda i: (0,) * a.ndim)
    return pl.pallas_call(
        functools.partial(_merge_kernel, final=final),
        grid=(n // tm,),
        in_specs=[row(SWA_WIDTH), row(MLA_WIDTH), row(D_MODEL), row(D_MODEL), row(D_MODEL),
                  row(PLE_DIM), full(wa), full(wb), full(wout), full(gple), full(wpg),
                  full(wpp), full(gfin)],
        out_specs=row(D_MODEL),
        out_shape=jax.ShapeDtypeStruct((n, D_MODEL), F32),
        compiler_params=pltpu.CompilerParams(dimension_semantics=("arbitrary",),
                                             vmem_limit_bytes=VMEM_LIMIT),
        name="merge",
    )(oa, ob, ma, mb, x2d, p2d, wa, wb, wout, gple, wpg, wpp, gfin)


def _pack_weights(w_in, w_uq, w_ukv):
    d = w_in.shape[0]
    kr0 = _C_KR
    kr = w_in[:, :, kr0:kr0 + MLA_ROPE]
    kr_pad = jnp.pad(kr, ((0, 0), (0, 0), (MLA_NOPE, LANES - MLA_QK)))
    win = jnp.concatenate([w_in[:, :, :kr0], kr_pad, w_in[:, :, kr0 + MLA_ROPE:]], axis=-1)
    wuq = w_uq.reshape(d, MLA_Q_LORA, MLA_HEADS, MLA_QK)
    wuq = jnp.pad(wuq, ((0, 0), (0, 0), (0, 0), (0, LANES - MLA_QK))).reshape(d, MLA_Q_LORA, MLA_PAD)
    wukv = w_ukv.reshape(d, MLA_KV_LORA, MLA_HEADS, MLA_NOPE + MLA_V)
    wuk = jnp.pad(wukv[..., :MLA_NOPE], ((0, 0), (0, 0), (0, 0), (0, LANES - MLA_NOPE)))
    wuk = wuk.reshape(d, MLA_KV_LORA, MLA_PAD)
    wv = wukv[..., MLA_NOPE:].reshape(d, MLA_KV_LORA, MLA_HEADS // 2, 2, MLA_V)
    zv = jnp.zeros_like(wv[:, :, :, 0])
    wuv = jnp.stack([jnp.concatenate([wv[:, :, :, 0], zv], axis=-1),
                     jnp.concatenate([zv, wv[:, :, :, 1]], axis=-1)],
                    axis=3).reshape(d, MLA_KV_LORA, MLA_PAD)
    return win.astype(BF16), wuq.astype(BF16), wuk.astype(BF16), wuv.astype(BF16)


def _rope_inv_tile():
    inv = ROPE_THETA ** (-jnp.arange(0, MLA_ROPE, 2, dtype=F32) / MLA_ROPE)
    z = jnp.zeros((MLA_NOPE,), F32)
    return jnp.concatenate([z, inv, inv, jnp.zeros((LANES - MLA_QK,), F32)])[None, :]


def kernel(x, p, positions, g_mix, w_in, sink, g_q, w_uq, g_kv, w_ukv, w_br_a, w_br_b, w_out,
           g_ple, w_ple_gate, w_ple_proj, g_final):
    b, s, _ = x.shape
    depth = w_in.shape[0]
    n = b * s
    tm_in, tm_merge, swa_tile, tq, tk = 256, 256, 512, 512, 512
    assert n % tm_in == 0 and n % tm_merge == 0 and s % swa_tile == 0 and s % tq == 0
    assert tk % tq == 0

    win, wuq, wuk, wuv = _pack_weights(w_in, w_uq, w_ukv)
    wa, wb, wout = w_br_a.astype(BF16), w_br_b.astype(BF16), w_out.astype(BF16)
    wpg, wpp = w_ple_gate.astype(BF16), w_ple_proj.astype(BF16)
    inv_tile = _rope_inv_tile()
    pos_in = positions.reshape(n // tm_in, tm_in // LANES, LANES)
    pos4 = positions.reshape(b, s // swa_tile, swa_tile // BLOCK, LANES)
    posh = positions.reshape(b, s // BLOCK, 1, LANES)
    gfin = g_final[None, :]

    x2d = x.reshape(n, D_MODEL)
    for i in range(depth):
        aq, ak, av, ag, bg, ma, mb, q, k, v = _in_proj(
            x2d, pos_in, inv_tile, g_mix[i][None, :], win[i], g_q[i][None, :], wuq[i],
            g_kv[i][None, :], wuk[i], wuv[i], tm=tm_in)
        r3 = lambda t: t.reshape(b, s, t.shape[-1])
        oa = _swa(sink[i], r3(aq), r3(ak), r3(av), r3(ag), pos4, posh, tile=swa_tile)
        ob = _mla(r3(q), r3(k), r3(v), r3(bg), tq=tq, tk=tk)
        x2d = _merge(oa.reshape(n, SWA_WIDTH), ob.reshape(n, MLA_WIDTH), ma, mb, x2d,
                     p[i].reshape(n, PLE_DIM), wa[i], wb[i], wout[i], g_ple[i][None, :], wpg[i],
                     wpp[i], gfin, tm=tm_merge, final=(i == depth - 1))
    return x2d.reshape(b, s, D_MODEL)
```

```python
import functools

import jax
import jax.numpy as jnp
from jax import lax
from jax.experimental import pallas as pl
from jax.experimental.pallas import tpu as pltpu

F32 = jnp.float32
BF16 = jnp.bfloat16

D_MODEL = 1024
PLE_DIM = 256
BLOCK = 128
LANES = 128
EPS = 1e-6
NEG = -1e30
LOG2E = 1.4426950408889634

SWA_HEADS = 8
SWA_KV_HEADS = 2
SWA_HEAD_DIM = 64
SWA_WIDTH = SWA_HEADS * SWA_HEAD_DIM
SWA_KV_WIDTH = SWA_KV_HEADS * SWA_HEAD_DIM

MLA_HEADS = 8
MLA_NOPE = 64
MLA_ROPE = 32
MLA_V = 64
MLA_Q_LORA = 256
MLA_KV_LORA = 128
MLA_WIDTH = MLA_HEADS * MLA_V
MLA_QK = MLA_NOPE + MLA_ROPE
MLA_PAD = MLA_HEADS * LANES
ROPE_THETA = 10000.0

_C_AQ = 0
_C_AK = _C_AQ + SWA_WIDTH
_C_AV = _C_AK + SWA_KV_WIDTH
_C_AG = _C_AV + SWA_KV_WIDTH
_C_QD = _C_AG + SWA_WIDTH
_C_KVD = _C_QD + MLA_Q_LORA
_C_KR = _C_KVD + MLA_KV_LORA
_C_BG = _C_KR + LANES
_C_MA = _C_BG + MLA_WIDTH
_C_MB = _C_MA + D_MODEL
_C_END = _C_MB + D_MODEL

_DIST_MASK = 2.0 ** 8 * 1e30

VMEM_LIMIT = 56 * 1024 * 1024


def _rms(x, g):
    return x * lax.rsqrt(jnp.mean(x * x, axis=-1, keepdims=True) + EPS) * g


def _sigmoid(x):
    return 1.0 / (1.0 + jnp.exp(-x))


def _col_bcast(row):
    return jnp.broadcast_to(row, (LANES, LANES)).T


def _dot(a, b):
    return jnp.dot(a, b, preferred_element_type=F32)


def _dot_nt(a, b):
    return lax.dot_general(a, b, (((1,), (1,)), ((), ())), preferred_element_type=F32)


def _in_proj_kernel(x_ref, pos_ref, inv_ref, gmix_ref, win_ref, gq_ref, wuq_ref, gkv_ref,
                    wuk_ref, wuv_ref,
                    aq_ref, ak_ref, av_ref, ag_ref, bg_ref, ma_ref, mb_ref, q_ref, k_ref, v_ref,
                    *, tm):
    h = _rms(x_ref[...], gmix_ref[...]).astype(BF16)

    def proj(lo, hi):
        return _dot(h, win_ref[:, lo:hi])

    qd = proj(_C_QD, _C_KVD)
    kvr = proj(_C_KVD, _C_BG)

    half = MLA_ROPE // 2
    grp = 64
    ngrp = tm // grp
    pos = pos_ref[0].astype(F32)
    cols = [_col_bcast(pos[g:g + 1, :]) for g in range(tm // LANES)]
    lane_c = lax.broadcasted_iota(jnp.int32, (grp, LANES), 1)
    posc = jnp.zeros((grp, LANES), F32)
    for g in range(ngrp):
        src = cols[(g * grp) // LANES][(g * grp) % LANES:(g * grp) % LANES + grp, :]
        posc = jnp.where(lane_c // half == g, src, posc)
    ang = posc * inv_ref[...]
    cos_c = jnp.cos(ang)
    sin_c = jnp.sin(ang)
    m_lo = (lane_c >= MLA_NOPE) & (lane_c < MLA_NOPE + half)
    m_hi = (lane_c >= MLA_NOPE + half) & (lane_c < MLA_QK)
    cos_g, slo_g, shi_g = [], [], []
    for g in range(ngrp):
        to_lo = (MLA_NOPE - half * g) % LANES
        to_hi = (MLA_NOPE + half - half * g) % LANES
        c_lo, c_hi = pltpu.roll(cos_c, to_lo, 1), pltpu.roll(cos_c, to_hi, 1)
        s_lo_r, s_hi_r = pltpu.roll(sin_c, to_lo, 1), pltpu.roll(sin_c, to_hi, 1)
        cos_g.append(jnp.where(m_lo, c_lo, jnp.where(m_hi, c_hi, 1.0)))
        slo_g.append(jnp.where(m_lo, -s_lo_r, 0.0))
        shi_g.append(jnp.where(m_hi, s_hi_r, 0.0))
    cos = jnp.concatenate(cos_g, axis=0)
    s_lo = jnp.concatenate(slo_g, axis=0)
    s_hi = jnp.concatenate(shi_g, axis=0)

    def rope(t, c, s1, s2):
        return (t * c + pltpu.roll(t, LANES - half, 1) * s1 + pltpu.roll(t, half, 1) * s2)

    qscale = MLA_QK ** -0.5 * LOG2E
    wide = 2 * LANES

    def akv_piece():
        akv = proj(_C_AK, _C_AG)
        ak_ref[...] = akv[:, :SWA_KV_WIDTH].astype(BF16)
        av_ref[...] = akv[:, SWA_KV_WIDTH:].astype(BF16)

    def piece(ref, col0, j):
        def run():
            ref[:, j * wide:(j + 1) * wide] = proj(col0 + j * wide, col0 + (j + 1) * wide).astype(BF16)
        return run

    pieces = [akv_piece]
    for ref, col0, width in ((aq_ref, _C_AQ, SWA_WIDTH), (ag_ref, _C_AG, SWA_WIDTH),
                             (bg_ref, _C_BG, MLA_WIDTH), (ma_ref, _C_MA, D_MODEL),
                             (mb_ref, _C_MB, D_MODEL)):
        pieces += [piece(ref, col0, j) for j in range(width // wide)]

    def some_pieces(n):
        for _ in range(min(n, len(pieces))):
            pieces.pop()()

    some_pieces(4)
    qn = _rms(qd, gq_ref[...]).astype(BF16)
    cq, s1q, s2q = cos * qscale, s_lo * qscale, s_hi * qscale
    for pair in range(MLA_HEADS // 2):
        t = _dot(qn, wuq_ref[:, 2 * pair * LANES:(2 * pair + 2) * LANES])
        for hf in range(2):
            c = slice((2 * pair + hf) * LANES, (2 * pair + hf + 1) * LANES)
            q_ref[:, c] = rope(t[:, hf * LANES:(hf + 1) * LANES], cq, s1q, s2q).astype(BF16)
        some_pieces(1)

    kvn = _rms(kvr[:, :MLA_KV_LORA], gkv_ref[...]).astype(BF16)
    kr = rope(kvr[:, MLA_KV_LORA:], cos, s_lo, s_hi)
    some_pieces(1)
    for pair in range(MLA_HEADS // 2):
        t = _dot(kvn, wuk_ref[:, 2 * pair * LANES:(2 * pair + 2) * LANES])
        for hf in range(2):
            c = slice((2 * pair + hf) * LANES, (2 * pair + hf + 1) * LANES)
            k_ref[:, c] = (t[:, hf * LANES:(hf + 1) * LANES] + kr).astype(BF16)
        some_pieces(1)
    for pair in range(MLA_HEADS // 2):
        c = slice(2 * pair * LANES, (2 * pair + 2) * LANES)
        v_ref[:, c] = _dot(kvn, wuv_ref[:, c]).astype(BF16)
        some_pieces(1)
    some_pieces(len(pieces))


def _in_proj(x2d, pos3, inv_tile, gmix, win, gq, wuq, gkv, wuk, wuv, *, tm):
    n = x2d.shape[0]
    grid = (n // tm,)
    row = lambda w: pl.BlockSpec((tm, w), lambda i: (i, 0))
    full = lambda a: pl.BlockSpec(a.shape, lambda i: (0,) * a.ndim)
    widths = (SWA_WIDTH, SWA_KV_WIDTH, SWA_KV_WIDTH, SWA_WIDTH, MLA_WIDTH, D_MODEL, D_MODEL,
              MLA_PAD, MLA_PAD, MLA_PAD)
    return pl.pallas_call(
        functools.partial(_in_proj_kernel, tm=tm),
        grid=grid,
        in_specs=[row(D_MODEL),
                  pl.BlockSpec((1, tm // LANES, LANES), lambda i: (i, 0, 0)),
                  full(inv_tile), full(gmix), full(win), full(gq), full(wuq), full(gkv),
                  full(wuk), full(wuv)],
        out_specs=[row(w) for w in widths],
        out_shape=[jax.ShapeDtypeStruct((n, w), BF16) for w in widths],
        compiler_params=pltpu.CompilerParams(dimension_semantics=("arbitrary",),
                                             vmem_limit_bytes=VMEM_LIMIT),
        name="in_proj",
    )(x2d, pos3, inv_tile, gmix, win, gq, wuq, gkv, wuk, wuv)


def _swa_kernel(sink_ref, q_ref, k_ref, kh_ref, v_ref, vh_ref, g_ref, pos_ref, posh_ref, o_ref,
                *, tile):
    i = pl.program_id(1)
    nsub = tile // BLOCK
    group = SWA_HEADS // SWA_KV_HEADS

    def variants(main_ref, halo_ref):
        t = jnp.concatenate([halo_ref[0], main_ref[0]], axis=0).astype(F32)
        r = pltpu.roll(t, SWA_HEAD_DIM, 1)
        lo = lax.broadcasted_iota(jnp.int32, t.shape, 1) < SWA_HEAD_DIM
        z = jnp.zeros_like(t)
        return [[jnp.where(lo, t, z).astype(BF16), jnp.where(lo, z, r).astype(BF16)],
                [jnp.where(lo, r, z).astype(BF16), jnp.where(lo, z, t).astype(BF16)]]

    kv = variants(k_ref, kh_ref)
    vv = variants(v_ref, vh_ref)
    pos_rows = jnp.concatenate([posh_ref[0, 0], pos_ref[0, 0]], axis=0)

    ii = lax.broadcasted_iota(jnp.int32, (BLOCK, 2 * BLOCK), 0)
    jj = lax.broadcasted_iota(jnp.int32, (BLOCK, 2 * BLOCK), 1)
    band = (jj > ii) & (jj <= ii + BLOCK)

    for sub in range(nsub):
        rows = slice(sub * BLOCK, (sub + 1) * BLOCK)
        krows = slice(sub * BLOCK, (sub + 2) * BLOCK)
        pq = _col_bcast(pos_rows[sub + 1:sub + 2, :])[:, 0:1]
        pk = jnp.concatenate([pos_rows[sub:sub + 1, :], pos_rows[sub + 1:sub + 2, :]], axis=1)
        dist = (pq - pk).astype(F32)
        valid = band
        if sub == 0:
            valid = band & ((jj >= BLOCK) | (i > 0))
        distm = jnp.where(valid, dist, _DIST_MASK)
        for t in range(SWA_HEADS // 2):
            g = (2 * t) // group
            qp = q_ref[0, rows, t * LANES:(t + 1) * LANES]
            acc = None
            for hf in range(2):
                hd = 2 * t + hf
                slope = 2.0 ** (-8.0 * (hd + 1) / SWA_HEADS)
                sink = sink_ref[hd]
                s = _dot_nt(qp, kv[g][hf][krows]) - slope * distm
                m = jnp.maximum(jnp.max(s, axis=-1, keepdims=True), sink)
                e = jnp.exp(s - m)
                den = jnp.sum(e, axis=-1, keepdims=True) + jnp.exp(sink - m)
                pv = _dot(e.astype(BF16), vv[g][hf][krows]) / den
                acc = pv if acc is None else acc + pv
            gate = g_ref[0, rows, t * LANES:(t + 1) * LANES].astype(F32)
            o_ref[0, rows, t * LANES:(t + 1) * LANES] = (acc * gate * _sigmoid(gate)).astype(BF16)


def _swa(sink, q, k, v, gate, pos4, posh, *, tile):
    b, s, _ = q.shape
    nsub = tile // BLOCK
    grid = (b, s // tile)
    main = lambda w: pl.BlockSpec((1, tile, w), lambda bi, i: (bi, i, 0))
    halo = pl.BlockSpec((1, BLOCK, SWA_KV_WIDTH),
                        lambda bi, i: (bi, jnp.maximum(i * nsub - 1, 0), 0))
    return pl.pallas_call(
        functools.partial(_swa_kernel, tile=tile),
        grid=grid,
        in_specs=[pl.BlockSpec(memory_space=pltpu.SMEM),
                  main(SWA_WIDTH), main(SWA_KV_WIDTH), halo, main(SWA_KV_WIDTH), halo,
                  main(SWA_WIDTH),
                  pl.BlockSpec((1, 1, nsub, LANES), lambda bi, i: (bi, i, 0, 0)),
                  pl.BlockSpec((1, 1, 1, LANES),
                               lambda bi, i: (bi, jnp.maximum(i * nsub - 1, 0), 0, 0))],
        out_specs=main(SWA_WIDTH),
        out_shape=jax.ShapeDtypeStruct((b, s, SWA_WIDTH), BF16),
        compiler_params=pltpu.CompilerParams(dimension_semantics=("arbitrary", "arbitrary"),
                                             vmem_limit_bytes=VMEM_LIMIT),
        name="swa",
    )(sink, q, k, k, v, v, gate, pos4, posh)


def _mla_kernel(q_ref, k_ref, v_ref, g_ref, o_ref, m_sc, acc_sc, p_sc, alpha_sc, *, tq, tk):
    qi = pl.program_id(1)
    half = tq // 2
    full = slice(0, tq)
    lo_row = lax.broadcasted_iota(jnp.int32, (1, LANES), 1) < MLA_V

    all_pairs = range(MLA_HEADS // 2)

    def scores(rows, start, width, mask_off, pairs=all_pairs, first=False):
        nr = rows.stop - rows.start
        lo = jnp.broadcast_to(lo_row, (nr, LANES))
        if mask_off is not None:
            row = lax.broadcasted_iota(jnp.int32, (nr, width), 0)
            col = lax.broadcasted_iota(jnp.int32, (nr, width), 1)
            visible = col <= row + mask_off
        for pair in pairs:
            alphas = []
            for hd in (2 * pair, 2 * pair + 1):
                c = slice(hd * LANES, (hd + 1) * LANES)
                s = _dot_nt(q_ref[0, rows, c], k_ref[0, pl.ds(start, width), c])
                if mask_off is not None:
                    s = jnp.where(visible, s, NEG)
                m_cur = jnp.max(s, axis=-1, keepdims=True)
                if first:
                    m_new = jnp.broadcast_to(m_cur, (nr, LANES))
                else:
                    m_prev = m_sc[hd, rows]
                    m_new = jnp.maximum(m_prev, m_cur)
                    alphas.append(jnp.exp2(m_prev - m_new))
                p_sc[hd, rows, 0:width] = jnp.exp2(
                    s - jnp.tile(m_new, (1, width // LANES))).astype(BF16)
                m_sc[hd, rows] = m_new
            if first:
                alpha_sc[pair, rows] = jnp.zeros((nr, LANES), F32)
            else:
                alpha_sc[pair, rows] = jnp.where(lo, alphas[0], alphas[1])

    def weighted(rows, start, width, pairs=all_pairs, first=False):
        ones = [jnp.broadcast_to(jnp.where(lo_row, 1.0, 0.0).astype(BF16), (width, LANES)),
                jnp.broadcast_to(jnp.where(lo_row, 0.0, 1.0).astype(BF16), (width, LANES))]
        for pair in pairs:
            pv = None
            for hf in range(2):
                hd = 2 * pair + hf
                c = slice(hd * LANES, (hd + 1) * LANES)
                vt = jnp.concatenate([v_ref[0, pl.ds(start, width), c], ones[hf]], axis=1)
                t = _dot(p_sc[hd, rows, 0:width], vt)
                pv = t if pv is None else pv + t
            if first:
                acc_sc[pair, rows] = pv
            else:
                acc_sc[pair, rows] = (jnp.tile(alpha_sc[pair, rows], (1, 2)) * acc_sc[pair, rows]
                                      + pv)

    top, bottom = slice(0, half), slice(half, tq)
    diag = pl.multiple_of(qi * tq, tq)

    @pl.when(qi == 0)
    def _():
        scores(top, 0, half, 0, first=True)
        scores(bottom, 0, tq, half, first=True)
        weighted(top, 0, half, first=True)
        weighted(bottom, 0, tq, first=True)

    @pl.when(qi > 0)
    def _():
        acc_sc[...] = jnp.zeros(acc_sc.shape, F32)
        scores(full, 0, tk, None, first=True)

        def body(j, c):
            for pair in all_pairs:
                weighted(full, pl.multiple_of((j - 1) * tk, tk), tk, (pair,))
                scores(full, pl.multiple_of(j * tk, tk), tk, None, (pair,))
            return c

        lax.fori_loop(1, qi, body, 0)
        prev = pl.multiple_of((qi - 1) * tk, tk)
        for pair in all_pairs:
            weighted(full, prev, tk, (pair,))
            scores(top, diag, half, 0, (pair,))
            scores(bottom, diag, tq, half, (pair,))
        weighted(top, diag, half)
        weighted(bottom, diag, tq)

    for pair in range(MLA_HEADS // 2):
        acc = acc_sc[pair]
        vcols = slice(pair * LANES, (pair + 1) * LANES)
        gate = g_ref[0, :, vcols].astype(F32)
        o_ref[0, :, vcols] = (acc[:, :LANES] / acc[:, LANES:] * gate * _sigmoid(gate)).astype(BF16)


def _mla(q, k, v, gate, *, tq, tk):
    b, s, _ = q.shape
    grid = (b, s // tq)
    blk = lambda w: pl.BlockSpec((1, tq, w), lambda bi, i: (bi, i, 0))
    seq = lambda w: pl.BlockSpec((1, s, w), lambda bi, i: (bi, 0, 0))
    return pl.pallas_call(
        functools.partial(_mla_kernel, tq=tq, tk=tk),
        grid=grid,
        in_specs=[blk(MLA_PAD), seq(MLA_PAD), seq(MLA_PAD), blk(MLA_WIDTH)],
        out_specs=blk(MLA_WIDTH),
        out_shape=jax.ShapeDtypeStruct((b, s, MLA_WIDTH), BF16),
        scratch_shapes=[pltpu.VMEM((MLA_HEADS, tq, LANES), F32),
                        pltpu.VMEM((MLA_HEADS // 2, tq, 2 * LANES), F32),
                        pltpu.VMEM((MLA_HEADS, tq, tk), BF16),
                        pltpu.VMEM((MLA_HEADS // 2, tq, LANES), F32)],
        compiler_params=pltpu.CompilerParams(dimension_semantics=("arbitrary", "arbitrary"),
                                             vmem_limit_bytes=VMEM_LIMIT),
        name="mla",
    )(q, k, v, gate)


def _merge_kernel(oa_ref, ob_ref, ma_ref, mb_ref, x_ref, p_ref, wa_ref, wb_ref, wout_ref,
                  gple_ref, wpg_ref, wpp_ref, gfin_ref, out_ref, *, final):
    ya = _dot(oa_ref[...], wa_ref[...])
    yb = _dot(ob_ref[...], wb_ref[...])
    y = _sigmoid(ma_ref[...].astype(F32)) * ya + _sigmoid(mb_ref[...].astype(F32)) * yb
    x1 = x_ref[...] + _dot(y.astype(BF16), wout_ref[...])
    hn = _rms(x1, gple_ref[...]).astype(BF16)
    pg = _sigmoid(_dot(hn, wpg_ref[...]))
    pp = _dot(p_ref[...].astype(BF16), wpp_ref[...])
    x2 = x1 + pg * pp
    if final:
        x2 = _rms(x2, gfin_ref[...])
    out_ref[...] = x2


def _merge(oa, ob, ma, mb, x2d, p2d, wa, wb, wout, gple, wpg, wpp, gfin, *, tm, final):
    n = x2d.shape[0]
    row = lambda w: pl.BlockSpec((tm, w), lambda i: (i, 0))
    full = lambda a: pl.BlockSpec(a.shape, lambda i: (0,) * a.ndim)
    return pl.pallas_call(
        functools.partial(_merge_kernel, final=final),
        grid=(n // tm,),
        in_specs=[row(SWA_WIDTH), row(MLA_WIDTH), row(D_MODEL), row(D_MODEL), row(D_MODEL),
                  row(PLE_DIM), full(wa), full(wb), full(wout), full(gple), full(wpg),
                  full(wpp), full(gfin)],
        out_specs=row(D_MODEL),
        out_shape=jax.ShapeDtypeStruct((n, D_MODEL), F32),
        compiler_params=pltpu.CompilerParams(dimension_semantics=("arbitrary",),
                                             vmem_limit_bytes=VMEM_LIMIT),
        name="merge",
    )(oa, ob, ma, mb, x2d, p2d, wa, wb, wout, gple, wpg, wpp, gfin)


def _pack_weights(w_in, w_uq, w_ukv):
    d = w_in.shape[0]
    kr0 = _C_KR
    kr = w_in[:, :, kr0:kr0 + MLA_ROPE]
    kr_pad = jnp.pad(kr, ((0, 0), (0, 0), (MLA_NOPE, LANES - MLA_QK)))
    aq = w_in[:, :, :_C_AK] * (SWA_HEAD_DIM ** -0.5)
    win = jnp.concatenate([aq, w_in[:, :, _C_AK:kr0], kr_pad, w_in[:, :, kr0 + MLA_ROPE:]], axis=-1)
    wuq = w_uq.reshape(d, MLA_Q_LORA, MLA_HEADS, MLA_QK)
    wuq = jnp.pad(wuq, ((0, 0), (0, 0), (0, 0), (0, LANES - MLA_QK))).reshape(d, MLA_Q_LORA, MLA_PAD)
    wukv = w_ukv.reshape(d, MLA_KV_LORA, MLA_HEADS, MLA_NOPE + MLA_V)
    wuk = jnp.pad(wukv[..., :MLA_NOPE], ((0, 0), (0, 0), (0, 0), (0, LANES - MLA_NOPE)))
    wuk = wuk.reshape(d, MLA_KV_LORA, MLA_PAD)
    wv = wukv[..., MLA_NOPE:].reshape(d, MLA_KV_LORA, MLA_HEADS // 2, 2, MLA_V)
    zv = jnp.zeros_like(wv[:, :, :, 0])
    wuv = jnp.stack([jnp.concatenate([wv[:, :, :, 0], zv], axis=-1),
                     jnp.concatenate([zv, wv[:, :, :, 1]], axis=-1)],
                    axis=3).reshape(d, MLA_KV_LORA, MLA_PAD)
    return win.astype(BF16), wuq.astype(BF16), wuk.astype(BF16), wuv.astype(BF16)


def _rope_inv_tile():
    inv = ROPE_THETA ** (-jnp.arange(0, MLA_ROPE, 2, dtype=F32) / MLA_ROPE)
    return jnp.tile(inv, LANES // (MLA_ROPE // 2))[None, :]


def kernel(x, p, positions, g_mix, w_in, sink, g_q, w_uq, g_kv, w_ukv, w_br_a, w_br_b, w_out,
           g_ple, w_ple_gate, w_ple_proj, g_final):
    b, s, _ = x.shape
    depth = w_in.shape[0]
    n = b * s
    tm_in, tm_merge, swa_tile, tq, tk = 512, 512, 512, 512, 512
    assert n % tm_in == 0 and n % tm_merge == 0 and s % swa_tile == 0 and s % tq == 0
    assert tk == tq
    assert tm_in % 64 == 0 and tm_in // 64 <= LANES // (MLA_ROPE // 2)

    win, wuq, wuk, wuv = _pack_weights(w_in, w_uq, w_ukv)
    wa, wb, wout = w_br_a.astype(BF16), w_br_b.astype(BF16), w_out.astype(BF16)
    wpg, wpp = w_ple_gate.astype(BF16), w_ple_proj.astype(BF16)
    inv_tile = _rope_inv_tile()
    pos_in = positions.reshape(n // tm_in, tm_in // LANES, LANES)
    pos4 = positions.reshape(b, s // swa_tile, swa_tile // BLOCK, LANES)
    posh = positions.reshape(b, s // BLOCK, 1, LANES)
    gfin = g_final[None, :]

    x2d = x.reshape(n, D_MODEL)
    for i in range(depth):
        aq, ak, av, ag, bg, ma, mb, q, k, v = _in_proj(
            x2d, pos_in, inv_tile, g_mix[i][None, :], win[i], g_q[i][None, :], wuq[i],
            g_kv[i][None, :], wuk[i], wuv[i], tm=tm_in)
        r3 = lambda t: t.reshape(b, s, t.shape[-1])
        oa = _swa(sink[i], r3(aq), r3(ak), r3(av), r3(ag), pos4, posh, tile=swa_tile)
        ob = _mla(r3(q), r3(k), r3(v), r3(bg), tq=tq, tk=tk)
        x2d = _merge(oa.reshape(n, SWA_WIDTH), ob.reshape(n, MLA_WIDTH), ma, mb, x2d,
                     p[i].reshape(n, PLE_DIM), wa[i], wb[i], wout[i], g_ple[i][None, :], wpg[i],
                     wpp[i], gfin, tm=tm_merge, final=(i == depth - 1))
    return x2d.reshape(b, s, D_MODEL)
```

```python
import functools

import jax
import jax.numpy as jnp
from jax import lax
from jax.experimental import pallas as pl
from jax.experimental.pallas import tpu as pltpu

F32 = jnp.float32
BF16 = jnp.bfloat16

D_MODEL = 1024
PLE_DIM = 256
BLOCK = 128
LANES = 128
EPS = 1e-6
NEG = -1e30
LOG2E = 1.4426950408889634

SWA_HEADS = 8
SWA_KV_HEADS = 2
SWA_HEAD_DIM = 64
SWA_WIDTH = SWA_HEADS * SWA_HEAD_DIM
SWA_KV_WIDTH = SWA_KV_HEADS * SWA_HEAD_DIM

MLA_HEADS = 8
MLA_NOPE = 64
MLA_ROPE = 32
MLA_V = 64
MLA_Q_LORA = 256
MLA_KV_LORA = 128
MLA_WIDTH = MLA_HEADS * MLA_V
MLA_QK = MLA_NOPE + MLA_ROPE
MLA_PAD = MLA_HEADS * LANES
ROPE_THETA = 10000.0

_C_AQ = 0
_C_AK = _C_AQ + SWA_WIDTH
_C_AV = _C_AK + SWA_KV_WIDTH
_C_AG = _C_AV + SWA_KV_WIDTH
_C_QD = _C_AG + SWA_WIDTH
_C_KVD = _C_QD + MLA_Q_LORA
_C_KR = _C_KVD + MLA_KV_LORA
_C_BG = _C_KR + LANES
_C_MA = _C_BG + MLA_WIDTH
_C_MB = _C_MA + D_MODEL
_C_END = _C_MB + D_MODEL

_DIST_MASK = 2.0 ** 8 * 1e30

VMEM_LIMIT = 56 * 1024 * 1024


def _rms(x, g):
    return x * lax.rsqrt(jnp.mean(x * x, axis=-1, keepdims=True) + EPS) * g


def _sigmoid(x):
    return 1.0 / (1.0 + jnp.exp(-x))


def _col_bcast(row):
    return jnp.broadcast_to(row, (LANES, LANES)).T


def _dot(a, b):
    return jnp.dot(a, b, preferred_element_type=F32)


def _dot_nt(a, b):
    return lax.dot_general(a, b, (((1,), (1,)), ((), ())), preferred_element_type=F32)


def _in_proj_kernel(x_ref, pos_ref, inv_ref, gmix_ref, win_ref, gq_ref, wuq_ref, gkv_ref,
                    wuk_ref, wuv_ref,
                    aq_ref, ak_ref, av_ref, ag_ref, bg_ref, ma_ref, mb_ref, q_ref, k_ref, v_ref,
                    *, tm):
    h = _rms(x_ref[...], gmix_ref[...]).astype(BF16)

    def proj(lo, hi):
        return _dot(h, win_ref[:, lo:hi])

    qd = proj(_C_QD, _C_KVD)
    kvr = proj(_C_KVD, _C_BG)

    half = MLA_ROPE // 2
    grp = 64
    ngrp = tm // grp
    pos = pos_ref[0].astype(F32)
    cols = [_col_bcast(pos[g:g + 1, :]) for g in range(tm // LANES)]
    lane_c = lax.broadcasted_iota(jnp.int32, (grp, LANES), 1)
    posc = jnp.zeros((grp, LANES), F32)
    for g in range(ngrp):
        src = cols[(g * grp) // LANES][(g * grp) % LANES:(g * grp) % LANES + grp, :]
        posc = jnp.where(lane_c // half == g, src, posc)
    ang = posc * inv_ref[...]
    cos_c = jnp.cos(ang)
    sin_c = jnp.sin(ang)
    m_lo = (lane_c >= MLA_NOPE) & (lane_c < MLA_NOPE + half)
    m_hi = (lane_c >= MLA_NOPE + half) & (lane_c < MLA_QK)
    cos_g, slo_g, shi_g = [], [], []
    for g in range(ngrp):
        to_lo = (MLA_NOPE - half * g) % LANES
        to_hi = (MLA_NOPE + half - half * g) % LANES
        c_lo, c_hi = pltpu.roll(cos_c, to_lo, 1), pltpu.roll(cos_c, to_hi, 1)
        s_lo_r, s_hi_r = pltpu.roll(sin_c, to_lo, 1), pltpu.roll(sin_c, to_hi, 1)
        cos_g.append(jnp.where(m_lo, c_lo, jnp.where(m_hi, c_hi, 1.0)))
        slo_g.append(jnp.where(m_lo, -s_lo_r, 0.0))
        shi_g.append(jnp.where(m_hi, s_hi_r, 0.0))
    cos = jnp.concatenate(cos_g, axis=0)
    s_lo = jnp.concatenate(slo_g, axis=0)
    s_hi = jnp.concatenate(shi_g, axis=0)

    def rope(t, c, s1, s2):
        return (t * c + pltpu.roll(t, LANES - half, 1) * s1 + pltpu.roll(t, half, 1) * s2)

    qscale = MLA_QK ** -0.5 * LOG2E
    wide = 2 * LANES

    def akv_piece():
        akv = proj(_C_AK, _C_AG)
        ak_ref[...] = akv[:, :SWA_KV_WIDTH].astype(BF16)
        av_ref[...] = akv[:, SWA_KV_WIDTH:].astype(BF16)

    def piece(ref, col0, j):
        def run():
            ref[:, j * wide:(j + 1) * wide] = proj(col0 + j * wide, col0 + (j + 1) * wide).astype(BF16)
        return run

    pieces = [akv_piece]
    for ref, col0, width in ((aq_ref, _C_AQ, SWA_WIDTH), (ag_ref, _C_AG, SWA_WIDTH),
                             (bg_ref, _C_BG, MLA_WIDTH), (ma_ref, _C_MA, D_MODEL),
                             (mb_ref, _C_MB, D_MODEL)):
        pieces += [piece(ref, col0, j) for j in range(width // wide)]

    def some_pieces(n):
        for _ in range(min(n, len(pieces))):
            pieces.pop()()

    some_pieces(4)
    qn = _rms(qd, gq_ref[...]).astype(BF16)
    cq, s1q, s2q = cos * qscale, s_lo * qscale, s_hi * qscale
    for pair in range(MLA_HEADS // 2):
        t = _dot(qn, wuq_ref[:, 2 * pair * LANES:(2 * pair + 2) * LANES])
        for hf in range(2):
            c = slice((2 * pair + hf) * LANES, (2 * pair + hf + 1) * LANES)
            q_ref[:, c] = rope(t[:, hf * LANES:(hf + 1) * LANES], cq, s1q, s2q).astype(BF16)
        some_pieces(1)

    kvn = _rms(kvr[:, :MLA_KV_LORA], gkv_ref[...]).astype(BF16)
    kr = rope(kvr[:, MLA_KV_LORA:], cos, s_lo, s_hi)
    some_pieces(1)
    for pair in range(MLA_HEADS // 2):
        t = _dot(kvn, wuk_ref[:, 2 * pair * LANES:(2 * pair + 2) * LANES])
        for hf in range(2):
            c = slice((2 * pair + hf) * LANES, (2 * pair + hf + 1) * LANES)
            k_ref[:, c] = (t[:, hf * LANES:(hf + 1) * LANES] + kr).astype(BF16)
        some_pieces(1)
    for pair in range(MLA_HEADS // 2):
        c = slice(2 * pair * LANES, (2 * pair + 2) * LANES)
        v_ref[:, c] = _dot(kvn, wuv_ref[:, c]).astype(BF16)
        some_pieces(1)
    some_pieces(len(pieces))


def _in_proj(x2d, pos3, inv_tile, gmix, win, gq, wuq, gkv, wuk, wuv, *, tm):
    n = x2d.shape[0]
    grid = (n // tm,)
    row = lambda w: pl.BlockSpec((tm, w), lambda i: (i, 0))
    full = lambda a: pl.BlockSpec(a.shape, lambda i: (0,) * a.ndim)
    widths = (SWA_WIDTH, SWA_KV_WIDTH, SWA_KV_WIDTH, SWA_WIDTH, MLA_WIDTH, D_MODEL, D_MODEL,
              MLA_PAD, MLA_PAD, MLA_PAD)
    return pl.pallas_call(
        functools.partial(_in_proj_kernel, tm=tm),
        grid=grid,
        in_specs=[row(D_MODEL),
                  pl.BlockSpec((1, tm // LANES, LANES), lambda i: (i, 0, 0)),
                  full(inv_tile), full(gmix), full(win), full(gq), full(wuq), full(gkv),
                  full(wuk), full(wuv)],
        out_specs=[row(w) for w in widths],
        out_shape=[jax.ShapeDtypeStruct((n, w), BF16) for w in widths],
        compiler_params=pltpu.CompilerParams(dimension_semantics=("arbitrary",),
                                             vmem_limit_bytes=VMEM_LIMIT),
        name="in_proj",
    )(x2d, pos3, inv_tile, gmix, win, gq, wuq, gkv, wuk, wuv)


def _swa_kernel(sink_ref, q_ref, k_ref, kh_ref, v_ref, vh_ref, g_ref, pos_ref, posh_ref, o_ref,
                *, tile):
    i = pl.program_id(1)
    nsub = tile // BLOCK
    group = SWA_HEADS // SWA_KV_HEADS

    def variants(main_ref, halo_ref):
        t = jnp.concatenate([halo_ref[0], main_ref[0]], axis=0).astype(F32)
        r = pltpu.roll(t, SWA_HEAD_DIM, 1)
        lo = lax.broadcasted_iota(jnp.int32, t.shape, 1) < SWA_HEAD_DIM
        z = jnp.zeros_like(t)
        return [[jnp.where(lo, t, z).astype(BF16), jnp.where(lo, z, r).astype(BF16)],
                [jnp.where(lo, r, z).astype(BF16), jnp.where(lo, z, t).astype(BF16)]]

    kv = variants(k_ref, kh_ref)
    vv = variants(v_ref, vh_ref)
    pos_rows = jnp.concatenate([posh_ref[0, 0], pos_ref[0, 0]], axis=0)

    ii = lax.broadcasted_iota(jnp.int32, (BLOCK, 2 * BLOCK), 0)
    jj = lax.broadcasted_iota(jnp.int32, (BLOCK, 2 * BLOCK), 1)
    band = (jj > ii) & (jj <= ii + BLOCK)

    for sub in range(nsub):
        rows = slice(sub * BLOCK, (sub + 1) * BLOCK)
        krows = slice(sub * BLOCK, (sub + 2) * BLOCK)
        pq = _col_bcast(pos_rows[sub + 1:sub + 2, :])[:, 0:1]
        pk = jnp.concatenate([pos_rows[sub:sub + 1, :], pos_rows[sub + 1:sub + 2, :]], axis=1)
        dist = (pq - pk).astype(F32)
        valid = band
        if sub == 0:
            valid = band & ((jj >= BLOCK) | (i > 0))
        distm = jnp.where(valid, dist, _DIST_MASK)
        for t in range(SWA_HEADS // 2):
            g = (2 * t) // group
            qp = q_ref[0, rows, t * LANES:(t + 1) * LANES]
            acc = None
            for hf in range(2):
                hd = 2 * t + hf
                slope = 2.0 ** (-8.0 * (hd + 1) / SWA_HEADS)
                sink = sink_ref[hd]
                s = _dot_nt(qp, kv[g][hf][krows]) - slope * distm
                m = jnp.maximum(jnp.max(s, axis=-1, keepdims=True), sink)
                e = jnp.exp(s - m)
                den = jnp.sum(e, axis=-1, keepdims=True) + jnp.exp(sink - m)
                pv = _dot(e.astype(BF16), vv[g][hf][krows]) / den
                acc = pv if acc is None else acc + pv
            gate = g_ref[0, rows, t * LANES:(t + 1) * LANES].astype(F32)
            o_ref[0, rows, t * LANES:(t + 1) * LANES] = (acc * gate * _sigmoid(gate)).astype(BF16)


def _swa(sink, q, k, v, gate, pos4, posh, *, tile):
    b, s, _ = q.shape
    nsub = tile // BLOCK
    grid = (b, s // tile)
    main = lambda w: pl.BlockSpec((1, tile, w), lambda bi, i: (bi, i, 0))
    halo = pl.BlockSpec((1, BLOCK, SWA_KV_WIDTH),
                        lambda bi, i: (bi, jnp.maximum(i * nsub - 1, 0), 0))
    return pl.pallas_call(
        functools.partial(_swa_kernel, tile=tile),
        grid=grid,
        in_specs=[pl.BlockSpec(memory_space=pltpu.SMEM),
                  main(SWA_WIDTH), main(SWA_KV_WIDTH), halo, main(SWA_KV_WIDTH), halo,
                  main(SWA_WIDTH),
                  pl.BlockSpec((1, 1, nsub, LANES), lambda bi, i: (bi, i, 0, 0)),
                  pl.BlockSpec((1, 1, 1, LANES),
                               lambda bi, i: (bi, jnp.maximum(i * nsub - 1, 0), 0, 0))],
        out_specs=main(SWA_WIDTH),
        out_shape=jax.ShapeDtypeStruct((b, s, SWA_WIDTH), BF16),
        compiler_params=pltpu.CompilerParams(dimension_semantics=("arbitrary", "arbitrary"),
                                             vmem_limit_bytes=VMEM_LIMIT),
        name="swa",
    )(sink, q, k, k, v, v, gate, pos4, posh)


def _mla_kernel(q_ref, k_ref, v_ref, g_ref, o_ref, m_sc, acc_sc, p_sc, alpha_sc, *, tq, tk):
    qi = pl.program_id(1)
    half = tq // 2
    full = slice(0, tq)
    lo_row = lax.broadcasted_iota(jnp.int32, (1, LANES), 1) < MLA_V

    all_pairs = range(MLA_HEADS // 2)

    def scores(rows, start, width, mask_off, pairs=all_pairs, first=False):
        nr = rows.stop - rows.start
        lo = jnp.broadcast_to(lo_row, (nr, LANES))
        if mask_off is not None:
            row = lax.broadcasted_iota(jnp.int32, (nr, width), 0)
            col = lax.broadcasted_iota(jnp.int32, (nr, width), 1)
            visible = col <= row + mask_off
        for pair in pairs:
            alphas = []
            for hd in (2 * pair, 2 * pair + 1):
                c = slice(hd * LANES, (hd + 1) * LANES)
                s = _dot_nt(q_ref[0, rows, c], k_ref[0, pl.ds(start, width), c])
                if mask_off is not None:
                    s = jnp.where(visible, s, NEG)
                m_cur = jnp.max(s, axis=-1, keepdims=True)
                if first:
                    m_new = jnp.broadcast_to(m_cur, (nr, LANES))
                else:
                    m_prev = m_sc[hd, rows]
                    m_new = jnp.maximum(m_prev, m_cur)
                    alphas.append(jnp.exp2(m_prev - m_new))
                p_sc[hd, rows, 0:width] = jnp.exp2(
                    s - jnp.tile(m_new, (1, width // LANES))).astype(BF16)
                m_sc[hd, rows] = m_new
            if first:
                alpha_sc[pair, rows] = jnp.zeros((nr, LANES), F32)
            else:
                alpha_sc[pair, rows] = jnp.where(lo, alphas[0], alphas[1])

    def weighted(rows, start, width, pairs=all_pairs, first=False):
        ones = [jnp.broadcast_to(jnp.where(lo_row, 1.0, 0.0).astype(BF16), (width, LANES)),
                jnp.broadcast_to(jnp.where(lo_row, 0.0, 1.0).astype(BF16), (width, LANES))]
        for pair in pairs:
            ps, vts = [], []
            for hf in range(2):
                hd = 2 * pair + hf
                c = slice(hd * LANES, (hd + 1) * LANES)
                ps.append(p_sc[hd, rows, 0:width])
                vts.append(jnp.concatenate([v_ref[0, pl.ds(start, width), c], ones[hf]], axis=1))
            pv = _dot(jnp.concatenate(ps, axis=1), jnp.concatenate(vts, axis=0))
            if first:
                acc_sc[pair, rows] = pv
            else:
                acc_sc[pair, rows] = (jnp.tile(alpha_sc[pair, rows], (1, 2)) * acc_sc[pair, rows]
                                      + pv)

    top, bottom = slice(0, half), slice(half, tq)
    diag = pl.multiple_of(qi * tq, tq)

    @pl.when(qi == 0)
    def _():
        scores(top, 0, half, 0, first=True)
        scores(bottom, 0, tq, half, first=True)
        weighted(top, 0, half, first=True)
        weighted(bottom, 0, tq, first=True)

    @pl.when(qi > 0)
    def _():
        acc_sc[...] = jnp.zeros(acc_sc.shape, F32)
        scores(full, 0, tk, None, first=True)

        def body(j, c):
            for pair in all_pairs:
                weighted(full, pl.multiple_of((j - 1) * tk, tk), tk, (pair,))
                scores(full, pl.multiple_of(j * tk, tk), tk, None, (pair,))
            return c

        lax.fori_loop(1, qi, body, 0)
        prev = pl.multiple_of((qi - 1) * tk, tk)
        for pair in all_pairs:
            weighted(full, prev, tk, (pair,))
            scores(top, diag, half, 0, (pair,))
            scores(bottom, diag, tq, half, (pair,))
        weighted(top, diag, half)
        weighted(bottom, diag, tq)

    for pair in range(MLA_HEADS // 2):
        acc = acc_sc[pair]
        vcols = slice(pair * LANES, (pair + 1) * LANES)
        gate = g_ref[0, :, vcols].astype(F32)
        o_ref[0, :, vcols] = (acc[:, :LANES] / acc[:, LANES:] * gate * _sigmoid(gate)).astype(BF16)


def _mla(q, k, v, gate, *, tq, tk):
    b, s, _ = q.shape
    grid = (b, s // tq)
    blk = lambda w: pl.BlockSpec((1, tq, w), lambda bi, i: (bi, i, 0))
    seq = lambda w: pl.BlockSpec((1, s, w), lambda bi, i: (bi, 0, 0))
    return pl.pallas_call(
        functools.partial(_mla_kernel, tq=tq, tk=tk),
        grid=grid,
        in_specs=[blk(MLA_PAD), seq(MLA_PAD), seq(MLA_PAD), blk(MLA_WIDTH)],
        out_specs=blk(MLA_WIDTH),
        out_shape=jax.ShapeDtypeStruct((b, s, MLA_WIDTH), BF16),
        scratch_shapes=[pltpu.VMEM((MLA_HEADS, tq, LANES), F32),
                        pltpu.VMEM((MLA_HEADS // 2, tq, 2 * LANES), F32),
                        pltpu.VMEM((MLA_HEADS, tq, tk), BF16),
                        pltpu.VMEM((MLA_HEADS // 2, tq, LANES), F32)],
        compiler_params=pltpu.CompilerParams(dimension_semantics=("arbitrary", "arbitrary"),
                                             vmem_limit_bytes=VMEM_LIMIT),
        name="mla",
    )(q, k, v, gate)


def _merge_kernel(oa_ref, ob_ref, ma_ref, mb_ref, x_ref, p_ref, wa_ref, wb_ref, wout_ref,
                  gple_ref, wpg_ref, wpp_ref, gfin_ref, out_ref, *, final):
    ya = _dot(oa_ref[...], wa_ref[...])
    yb = _dot(ob_ref[...], wb_ref[...])
    y = _sigmoid(ma_ref[...].astype(F32)) * ya + _sigmoid(mb_ref[...].astype(F32)) * yb
    x1 = x_ref[...] + _dot(y.astype(BF16), wout_ref[...])
    hn = _rms(x1, gple_ref[...]).astype(BF16)
    pg = _sigmoid(_dot(hn, wpg_ref[...]))
    pp = _dot(p_ref[...].astype(BF16), wpp_ref[...])
    x2 = x1 + pg * pp
    if final:
        x2 = _rms(x2, gfin_ref[...])
    out_ref[...] = x2


def _merge(oa, ob, ma, mb, x2d, p2d, wa, wb, wout, gple, wpg, wpp, gfin, *, tm, final):
    n = x2d.shape[0]
    row = lambda w: pl.BlockSpec((tm, w), lambda i: (i, 0))
    full = lambda a: pl.BlockSpec(a.shape, lambda i: (0,) * a.ndim)
    return pl.pallas_call(
        functools.partial(_merge_kernel, final=final),
        grid=(n // tm,),
        in_specs=[row(SWA_WIDTH), row(MLA_WIDTH), row(D_MODEL), row(D_MODEL), row(D_MODEL),
                  row(PLE_DIM), full(wa), full(wb), full(wout), full(gple), full(wpg),
                  full(wpp), full(gfin)],
        out_specs=row(D_MODEL),
        out_shape=jax.ShapeDtypeStruct((n, D_MODEL), F32),
        compiler_params=pltpu.CompilerParams(dimension_semantics=("arbitrary",),
                                             vmem_limit_bytes=VMEM_LIMIT),
        name="merge",
    )(oa, ob, ma, mb, x2d, p2d, wa, wb, wout, gple, wpg, wpp, gfin)


def _pack_weights(w_in, w_uq, w_ukv):
    d = w_in.shape[0]
    kr0 = _C_KR
    kr = w_in[:, :, kr0:kr0 + MLA_ROPE]
    kr_pad = jnp.pad(kr, ((0, 0), (0, 0), (MLA_NOPE, LANES - MLA_QK)))
    aq = w_in[:, :, :_C_AK] * (SWA_HEAD_DIM ** -0.5)
    win = jnp.concatenate([aq, w_in[:, :, _C_AK:kr0], kr_pad, w_in[:, :, kr0 + MLA_ROPE:]], axis=-1)
    wuq = w_uq.reshape(d, MLA_Q_LORA, MLA_HEADS, MLA_QK)
    wuq = jnp.pad(wuq, ((0, 0), (0, 0), (0, 0), (0, LANES - MLA_QK))).reshape(d, MLA_Q_LORA, MLA_PAD)
    wukv = w_ukv.reshape(d, MLA_KV_LORA, MLA_HEADS, MLA_NOPE + MLA_V)
    wuk = jnp.pad(wukv[..., :MLA_NOPE], ((0, 0), (0, 0), (0, 0), (0, LANES - MLA_NOPE)))
    wuk = wuk.reshape(d, MLA_KV_LORA, MLA_PAD)
    wv = wukv[..., MLA_NOPE:].reshape(d, MLA_KV_LORA, MLA_HEADS // 2, 2, MLA_V)
    zv = jnp.zeros_like(wv[:, :, :, 0])
    wuv = jnp.stack([jnp.concatenate([wv[:, :, :, 0], zv], axis=-1),
                     jnp.concatenate([zv, wv[:, :, :, 1]], axis=-1)],
                    axis=3).reshape(d, MLA_KV_LORA, MLA_PAD)
    return win.astype(BF16), wuq.astype(BF16), wuk.astype(BF16), wuv.astype(BF16)


def _rope_inv_tile():
    inv = ROPE_THETA ** (-jnp.arange(0, MLA_ROPE, 2, dtype=F32) / MLA_ROPE)
    return jnp.tile(inv, LANES // (MLA_ROPE // 2))[None, :]


def kernel(x, p, positions, g_mix, w_in, sink, g_q, w_uq, g_kv, w_ukv, w_br_a, w_br_b, w_out,
           g_ple, w_ple_gate, w_ple_proj, g_final):
    b, s, _ = x.shape
    depth = w_in.shape[0]
    n = b * s
    tm_in, tm_merge, swa_tile, tq, tk = 512, 512, 512, 512, 512
    assert n % tm_in == 0 and n % tm_merge == 0 and s % swa_tile == 0 and s % tq == 0
    assert tk == tq
    assert tm_in % 64 == 0 and tm_in // 64 <= LANES // (MLA_ROPE // 2)

    win, wuq, wuk, wuv = _pack_weights(w_in, w_uq, w_ukv)
    wa, wb, wout = w_br_a.astype(BF16), w_br_b.astype(BF16), w_out.astype(BF16)
    wpg, wpp = w_ple_gate.astype(BF16), w_ple_proj.astype(BF16)
    inv_tile = _rope_inv_tile()
    pos_in = positions.reshape(n // tm_in, tm_in // LANES, LANES)
    pos4 = positions.reshape(b, s // swa_tile, swa_tile // BLOCK, LANES)
    posh = positions.reshape(b, s // BLOCK, 1, LANES)
    gfin = g_final[None, :]

    x2d = x.reshape(n, D_MODEL)
    for i in range(depth):
        aq, ak, av, ag, bg, ma, mb, q, k, v = _in_proj(
            x2d, pos_in, inv_tile, g_mix[i][None, :], win[i], g_q[i][None, :], wuq[i],
            g_kv[i][None, :], wuk[i], wuv[i], tm=tm_in)
        r3 = lambda t: t.reshape(b, s, t.shape[-1])
        oa = _swa(sink[i], r3(aq), r3(ak), r3(av), r3(ag), pos4, posh, tile=swa_tile)
        ob = _mla(r3(q), r3(k), r3(v), r3(bg), tq=tq, tk=tk)
        x2d = _merge(oa.reshape(n, SWA_WIDTH), ob.reshape(n, MLA_WIDTH), ma, mb, x2d,
                     p[i].reshape(n, PLE_DIM), wa[i], wb[i], wout[i], g_ple[i][None, :], wpg[i],
                     wpp[i], gfin, tm=tm_merge, final=(i == depth - 1))
    return x2d.reshape(b, s, D_MODEL)
```

```python
import functools

import jax
import jax.numpy as jnp
from jax import lax
from jax.experimental import pallas as pl
from jax.experimental.pallas import tpu as pltpu

F32 = jnp.float32
BF16 = jnp.bfloat16

D_MODEL = 1024
PLE_DIM = 256
BLOCK = 128
LANES = 128
EPS = 1e-6
NEG = -1e30
LOG2E = 1.4426950408889634

SWA_HEADS = 8
SWA_KV_HEADS = 2
SWA_HEAD_DIM = 64
SWA_WIDTH = SWA_HEADS * SWA_HEAD_DIM
SWA_KV_WIDTH = SWA_KV_HEADS * SWA_HEAD_DIM

MLA_HEADS = 8
MLA_NOPE = 64
MLA_ROPE = 32
MLA_V = 64
MLA_Q_LORA = 256
MLA_KV_LORA = 128
MLA_WIDTH = MLA_HEADS * MLA_V
MLA_QK = MLA_NOPE + MLA_ROPE
MLA_PAD = MLA_HEADS * LANES
ROPE_THETA = 10000.0

_C_AQ = 0
_C_AK = _C_AQ + SWA_WIDTH
_C_AV = _C_AK + SWA_KV_WIDTH
_C_AG = _C_AV + SWA_KV_WIDTH
_C_QD = _C_AG + SWA_WIDTH
_C_KVD = _C_QD + MLA_Q_LORA
_C_KR = _C_KVD + MLA_KV_LORA
_C_BG = _C_KR + LANES
_C_MA = _C_BG + MLA_WIDTH
_C_MB = _C_MA + D_MODEL
_C_END = _C_MB + D_MODEL

_DIST_MASK = 2.0 ** 8 * 1e30

VMEM_LIMIT = 56 * 1024 * 1024


def _rms(x, g):
    return x * lax.rsqrt(jnp.mean(x * x, axis=-1, keepdims=True) + EPS) * g


def _sigmoid(x):
    return 1.0 / (1.0 + jnp.exp(-x))


def _col_bcast(row):
    return jnp.broadcast_to(row, (LANES, LANES)).T


def _dot(a, b):
    return jnp.dot(a, b, preferred_element_type=F32)


def _dot_nt(a, b):
    return lax.dot_general(a, b, (((1,), (1,)), ((), ())), preferred_element_type=F32)


def _in_proj_kernel(x_ref, pos_ref, inv_ref, gmix_ref, win_ref, gq_ref, wuq_ref, gkv_ref,
                    wuk_ref, wuv_ref,
                    aq_ref, ak_ref, av_ref, ag_ref, bg_ref, ma_ref, mb_ref, q_ref, k_ref, v_ref,
                    wpad_sc, *, tm):
    @pl.when(pl.program_id(0) == 0)
    def _():
        wpad_sc[:, :_C_KR] = win_ref[:, :_C_KR]
        t = win_ref[:, _C_KR:_C_KR + LANES].astype(F32)
        lane = lax.broadcasted_iota(jnp.int32, t.shape, 1)
        kr_lanes = (lane >= MLA_NOPE) & (lane < MLA_QK)
        wpad_sc[:, _C_KR:_C_BG] = jnp.where(kr_lanes, pltpu.roll(t, MLA_NOPE, 1), 0.0).astype(BF16)
        wpad_sc[:, _C_BG:] = win_ref[:, _C_KR + MLA_ROPE:]

    h = _rms(x_ref[...], gmix_ref[...]).astype(BF16)

    def proj(lo, hi):
        return _dot(h, wpad_sc[:, lo:hi])

    qd = proj(_C_QD, _C_KVD)
    kvr = proj(_C_KVD, _C_BG)

    half = MLA_ROPE // 2
    grp = 64
    ngrp = tm // grp
    pos = pos_ref[0].astype(F32)
    cols = [_col_bcast(pos[g:g + 1, :]) for g in range(tm // LANES)]
    lane_c = lax.broadcasted_iota(jnp.int32, (grp, LANES), 1)
    posc = jnp.zeros((grp, LANES), F32)
    for g in range(ngrp):
        src = cols[(g * grp) // LANES][(g * grp) % LANES:(g * grp) % LANES + grp, :]
        posc = jnp.where(lane_c // half == g, src, posc)
    ang = posc * inv_ref[...]
    cos_c = jnp.cos(ang)
    sin_c = jnp.sin(ang)
    m_lo = (lane_c >= MLA_NOPE) & (lane_c < MLA_NOPE + half)
    m_hi = (lane_c >= MLA_NOPE + half) & (lane_c < MLA_QK)
    cos_g, slo_g, shi_g = [], [], []
    for g in range(ngrp):
        to_lo = (MLA_NOPE - half * g) % LANES
        to_hi = (MLA_NOPE + half - half * g) % LANES
        c_lo, c_hi = pltpu.roll(cos_c, to_lo, 1), pltpu.roll(cos_c, to_hi, 1)
        s_lo_r, s_hi_r = pltpu.roll(sin_c, to_lo, 1), pltpu.roll(sin_c, to_hi, 1)
        cos_g.append(jnp.where(m_lo, c_lo, jnp.where(m_hi, c_hi, 1.0)))
        slo_g.append(jnp.where(m_lo, -s_lo_r, 0.0))
        shi_g.append(jnp.where(m_hi, s_hi_r, 0.0))
    cos = jnp.concatenate(cos_g, axis=0)
    s_lo = jnp.concatenate(slo_g, axis=0)
    s_hi = jnp.concatenate(shi_g, axis=0)

    def rope(t, c, s1, s2):
        return (t * c + pltpu.roll(t, LANES - half, 1) * s1 + pltpu.roll(t, half, 1) * s2)

    qscale = MLA_QK ** -0.5 * LOG2E
    wide = 2 * LANES

    def akv_piece():
        akv = proj(_C_AK, _C_AG)
        ak_ref[...] = akv[:, :SWA_KV_WIDTH].astype(BF16)
        av_ref[...] = akv[:, SWA_KV_WIDTH:].astype(BF16)

    def piece(ref, col0, j, scale=None):
        def run():
            t = proj(col0 + j * wide, col0 + (j + 1) * wide)
            ref[:, j * wide:(j + 1) * wide] = (t if scale is None else t * scale).astype(BF16)
        return run

    aq_scale = SWA_HEAD_DIM ** -0.5 * LOG2E
    pieces = [akv_piece] + [piece(aq_ref, _C_AQ, j, aq_scale) for j in range(SWA_WIDTH // wide)]
    for ref, col0, width in ((ag_ref, _C_AG, SWA_WIDTH), (bg_ref, _C_BG, MLA_WIDTH),
                             (ma_ref, _C_MA, D_MODEL), (mb_ref, _C_MB, D_MODEL)):
        pieces += [piece(ref, col0, j) for j in range(width // wide)]

    def some_pieces(n):
        for _ in range(min(n, len(pieces))):
            pieces.pop()()

    some_pieces(4)
    qn = _rms(qd, gq_ref[...]).astype(BF16)
    cq, s1q, s2q = cos * qscale, s_lo * qscale, s_hi * qscale
    for pair in range(MLA_HEADS // 2):
        t = _dot(qn, wuq_ref[:, 2 * pair * LANES:(2 * pair + 2) * LANES])
        for hf in range(2):
            c = slice((2 * pair + hf) * LANES, (2 * pair + hf + 1) * LANES)
            q_ref[:, c] = rope(t[:, hf * LANES:(hf + 1) * LANES], cq, s1q, s2q).astype(BF16)
        some_pieces(1)

    kvn = _rms(kvr[:, :MLA_KV_LORA], gkv_ref[...]).astype(BF16)
    kr = rope(kvr[:, MLA_KV_LORA:], cos, s_lo, s_hi)
    some_pieces(1)
    for pair in range(MLA_HEADS // 2):
        t = _dot(kvn, wuk_ref[:, 2 * pair * LANES:(2 * pair + 2) * LANES])
        for hf in range(2):
            c = slice((2 * pair + hf) * LANES, (2 * pair + hf + 1) * LANES)
            k_ref[:, c] = (t[:, hf * LANES:(hf + 1) * LANES] + kr).astype(BF16)
        some_pieces(1)
    for pair in range(MLA_HEADS // 2):
        c = slice(2 * pair * LANES, (2 * pair + 2) * LANES)
        v_ref[:, c] = _dot(kvn, wuv_ref[:, c]).astype(BF16)
        some_pieces(1)
    some_pieces(len(pieces))


def _in_proj(x2d, pos3, inv_tile, gmix, win, gq, wuq, gkv, wuk, wuv, *, tm):
    n = x2d.shape[0]
    grid = (n // tm,)
    row = lambda w: pl.BlockSpec((tm, w), lambda i: (i, 0))
    full = lambda a: pl.BlockSpec(a.shape, lambda i: (0,) * a.ndim)
    widths = (SWA_WIDTH, SWA_KV_WIDTH, SWA_KV_WIDTH, SWA_WIDTH, MLA_WIDTH, D_MODEL, D_MODEL,
              MLA_PAD, MLA_PAD, MLA_PAD)
    return pl.pallas_call(
        functools.partial(_in_proj_kernel, tm=tm),
        grid=grid,
        in_specs=[row(D_MODEL),
                  pl.BlockSpec((1, tm // LANES, LANES), lambda i: (i, 0, 0)),
                  full(inv_tile), full(gmix), full(win), full(gq), full(wuq), full(gkv),
                  full(wuk), full(wuv)],
        out_specs=[row(w) for w in widths],
        out_shape=[jax.ShapeDtypeStruct((n, w), BF16) for w in widths],
        scratch_shapes=[pltpu.VMEM((D_MODEL, _C_END), BF16)],
        compiler_params=pltpu.CompilerParams(dimension_semantics=("arbitrary",),
                                             vmem_limit_bytes=VMEM_LIMIT),
        name="in_proj",
    )(x2d, pos3, inv_tile, gmix, win, gq, wuq, gkv, wuk, wuv)


def _swa_kernel(sink_ref, q_ref, k_ref, kh_ref, v_ref, vh_ref, g_ref, pos_ref, posh_ref, o_ref,
                *, tile):
    i = pl.program_id(1)
    nsub = tile // BLOCK
    group = SWA_HEADS // SWA_KV_HEADS

    def variants(main_ref, halo_ref):
        t = jnp.concatenate([halo_ref[0], main_ref[0]], axis=0).astype(F32)
        r = pltpu.roll(t, SWA_HEAD_DIM, 1)
        lo = lax.broadcasted_iota(jnp.int32, t.shape, 1) < SWA_HEAD_DIM
        z = jnp.zeros_like(t)
        return [[jnp.where(lo, t, z).astype(BF16), jnp.where(lo, z, r).astype(BF16)],
                [jnp.where(lo, r, z).astype(BF16), jnp.where(lo, z, t).astype(BF16)]]

    kv = variants(k_ref, kh_ref)
    vv = variants(v_ref, vh_ref)
    pos_rows = jnp.concatenate([posh_ref[0, 0], pos_ref[0, 0]], axis=0)

    ii = lax.broadcasted_iota(jnp.int32, (BLOCK, 2 * BLOCK), 0)
    jj = lax.broadcasted_iota(jnp.int32, (BLOCK, 2 * BLOCK), 1)
    band = (jj > ii) & (jj <= ii + BLOCK)

    for sub in range(nsub):
        rows = slice(sub * BLOCK, (sub + 1) * BLOCK)
        krows = slice(sub * BLOCK, (sub + 2) * BLOCK)
        pq = _col_bcast(pos_rows[sub + 1:sub + 2, :])[:, 0:1]
        pk = jnp.concatenate([pos_rows[sub:sub + 1, :], pos_rows[sub + 1:sub + 2, :]], axis=1)
        dist = (pq - pk).astype(F32)
        valid = band
        if sub == 0:
            valid = band & ((jj >= BLOCK) | (i > 0))
        distm = jnp.where(valid, dist, _DIST_MASK)
        for t in range(SWA_HEADS // 2):
            g = (2 * t) // group
            qp = q_ref[0, rows, t * LANES:(t + 1) * LANES]
            acc = None
            for hf in range(2):
                hd = 2 * t + hf
                slope = 2.0 ** (-8.0 * (hd + 1) / SWA_HEADS) * LOG2E
                sink = sink_ref[hd] * LOG2E
                s = _dot_nt(qp, kv[g][hf][krows]) - slope * distm
                m = jnp.maximum(jnp.max(s, axis=-1, keepdims=True), sink)
                e = jnp.exp2(s - m)
                den = jnp.sum(e, axis=-1, keepdims=True) + jnp.exp2(sink - m)
                pv = _dot(e.astype(BF16), vv[g][hf][krows]) / den
                acc = pv if acc is None else acc + pv
            gate = g_ref[0, rows, t * LANES:(t + 1) * LANES].astype(F32)
            o_ref[0, rows, t * LANES:(t + 1) * LANES] = (acc * gate * _sigmoid(gate)).astype(BF16)


def _swa(sink, q, k, v, gate, pos4, posh, *, tile):
    b, s, _ = q.shape
    nsub = tile // BLOCK
    grid = (b, s // tile)
    main = lambda w: pl.BlockSpec((1, tile, w), lambda bi, i: (bi, i, 0))
    halo = pl.BlockSpec((1, BLOCK, SWA_KV_WIDTH),
                        lambda bi, i: (bi, jnp.maximum(i * nsub - 1, 0), 0))
    return pl.pallas_call(
        functools.partial(_swa_kernel, tile=tile),
        grid=grid,
        in_specs=[pl.BlockSpec(memory_space=pltpu.SMEM),
                  main(SWA_WIDTH), main(SWA_KV_WIDTH), halo, main(SWA_KV_WIDTH), halo,
                  main(SWA_WIDTH),
                  pl.BlockSpec((1, 1, nsub, LANES), lambda bi, i: (bi, i, 0, 0)),
                  pl.BlockSpec((1, 1, 1, LANES),
                               lambda bi, i: (bi, jnp.maximum(i * nsub - 1, 0), 0, 0))],
        out_specs=main(SWA_WIDTH),
        out_shape=jax.ShapeDtypeStruct((b, s, SWA_WIDTH), BF16),
        compiler_params=pltpu.CompilerParams(dimension_semantics=("arbitrary", "arbitrary"),
                                             vmem_limit_bytes=VMEM_LIMIT),
        name="swa",
    )(sink, q, k, k, v, v, gate, pos4, posh)


def _mla_kernel(q_ref, k_ref, v_ref, g_ref, o_ref, m_sc, acc_sc, p_sc, alpha_sc, *, tq, tk):
    qi = pl.program_id(1)
    half = tq // 2
    full = slice(0, tq)
    lo_row = lax.broadcasted_iota(jnp.int32, (1, LANES), 1) < MLA_V

    all_pairs = range(MLA_HEADS // 2)

    def scores(rows, start, width, mask_off, pairs=all_pairs, first=False):
        nr = rows.stop - rows.start
        lo = jnp.broadcast_to(lo_row, (nr, LANES))
        if mask_off is not None:
            row = lax.broadcasted_iota(jnp.int32, (nr, width), 0)
            col = lax.broadcasted_iota(jnp.int32, (nr, width), 1)
            visible = col <= row + mask_off
        for pair in pairs:
            alphas = []
            for hd in (2 * pair, 2 * pair + 1):
                c = slice(hd * LANES, (hd + 1) * LANES)
                s = _dot_nt(q_ref[0, rows, c], k_ref[0, pl.ds(start, width), c])
                if mask_off is not None:
                    s = jnp.where(visible, s, NEG)
                m_cur = jnp.max(s, axis=-1, keepdims=True)
                if first:
                    m_new = jnp.broadcast_to(m_cur, (nr, LANES))
                else:
                    m_prev = m_sc[hd, rows]
                    m_new = jnp.maximum(m_prev, m_cur)
                    alphas.append(jnp.exp2(m_prev - m_new))
                p_sc[hd, rows, 0:width] = jnp.exp2(
                    s - jnp.tile(m_new, (1, width // LANES))).astype(BF16)
                m_sc[hd, rows] = m_new
            if first:
                alpha_sc[pair, rows] = jnp.zeros((nr, LANES), F32)
            else:
                alpha_sc[pair, rows] = jnp.where(lo, alphas[0], alphas[1])

    def weighted(rows, start, width, pairs=all_pairs, first=False):
        ones = [jnp.broadcast_to(jnp.where(lo_row, 1.0, 0.0).astype(BF16), (width, LANES)),
                jnp.broadcast_to(jnp.where(lo_row, 0.0, 1.0).astype(BF16), (width, LANES))]
        for pair in pairs:
            ps, vts = [], []
            for hf in range(2):
                hd = 2 * pair + hf
                c = slice(hd * LANES, (hd + 1) * LANES)
                ps.append(p_sc[hd, rows, 0:width])
                vts.append(jnp.concatenate([v_ref[0, pl.ds(start, width), c], ones[hf]], axis=1))
            pv = _dot(jnp.concatenate(ps, axis=1), jnp.concatenate(vts, axis=0))
            if first:
                acc_sc[pair, rows] = pv
            else:
                acc_sc[pair, rows] = (jnp.tile(alpha_sc[pair, rows], (1, 2)) * acc_sc[pair, rows]
                                      + pv)

    top, bottom = slice(0, half), slice(half, tq)
    diag = pl.multiple_of(qi * tq, tq)

    @pl.when(qi == 0)
    def _():
        scores(top, 0, half, 0, first=True)
        scores(bottom, 0, tq, half, first=True)
        weighted(top, 0, half, first=True)
        weighted(bottom, 0, tq, first=True)

    @pl.when(qi > 0)
    def _():
        acc_sc[...] = jnp.zeros(acc_sc.shape, F32)
        scores(full, 0, tk, None, first=True)

        def body(j, c):
            for pair in all_pairs:
                weighted(full, pl.multiple_of((j - 1) * tk, tk), tk, (pair,))
                scores(full, pl.multiple_of(j * tk, tk), tk, None, (pair,))
            return c

        lax.fori_loop(1, qi, body, 0)
        prev = pl.multiple_of((qi - 1) * tk, tk)
        for pair in all_pairs:
            weighted(full, prev, tk, (pair,))
            scores(top, diag, half, 0, (pair,))
            scores(bottom, diag, tq, half, (pair,))
        weighted(top, diag, half)
        weighted(bottom, diag, tq)

    for pair in range(MLA_HEADS // 2):
        acc = acc_sc[pair]
        vcols = slice(pair * LANES, (pair + 1) * LANES)
        gate = g_ref[0, :, vcols].astype(F32)
        o_ref[0, :, vcols] = (acc[:, :LANES] / acc[:, LANES:] * gate * _sigmoid(gate)).astype(BF16)


def _mla(q, k, v, gate, *, tq, tk):
    b, s, _ = q.shape
    grid = (b, s // tq)
    blk = lambda w: pl.BlockSpec((1, tq, w), lambda bi, i: (bi, i, 0))
    seq = lambda w: pl.BlockSpec((1, s, w), lambda bi, i: (bi, 0, 0))
    return pl.pallas_call(
        functools.partial(_mla_kernel, tq=tq, tk=tk),
        grid=grid,
        in_specs=[blk(MLA_PAD), seq(MLA_PAD), seq(MLA_PAD), blk(MLA_WIDTH)],
        out_specs=blk(MLA_WIDTH),
        out_shape=jax.ShapeDtypeStruct((b, s, MLA_WIDTH), BF16),
        scratch_shapes=[pltpu.VMEM((MLA_HEADS, tq, LANES), F32),
                        pltpu.VMEM((MLA_HEADS // 2, tq, 2 * LANES), F32),
                        pltpu.VMEM((MLA_HEADS, tq, tk), BF16),
                        pltpu.VMEM((MLA_HEADS // 2, tq, LANES), F32)],
        compiler_params=pltpu.CompilerParams(dimension_semantics=("arbitrary", "arbitrary"),
                                             vmem_limit_bytes=VMEM_LIMIT),
        name="mla",
    )(q, k, v, gate)


def _merge_kernel(oa_ref, ob_ref, ma_ref, mb_ref, x_ref, p_ref, wa_ref, wb_ref, wout_ref,
                  gple_ref, wpg_ref, wpp_ref, gfin_ref, out_ref, *, final):
    ya = _dot(oa_ref[...], wa_ref[...])
    yb = _dot(ob_ref[...], wb_ref[...])
    y = _sigmoid(ma_ref[...].astype(F32)) * ya + _sigmoid(mb_ref[...].astype(F32)) * yb
    x1 = x_ref[...] + _dot(y.astype(BF16), wout_ref[...])
    pp = _dot(p_ref[...].astype(BF16), wpp_ref[...])
    hn = _rms(x1, gple_ref[...]).astype(BF16)
    wide = 2 * LANES
    pieces = []
    for j in range(D_MODEL // wide):
        c = slice(j * wide, (j + 1) * wide)
        x2 = x1[:, c] + _sigmoid(_dot(hn, wpg_ref[:, c])) * pp[:, c]
        if final:
            pieces.append(x2)
        else:
            out_ref[:, c] = x2
    if final:
        out_ref[...] = _rms(jnp.concatenate(pieces, axis=1), gfin_ref[...])


def _merge(oa, ob, ma, mb, x2d, p2d, wa, wb, wout, gple, wpg, wpp, gfin, *, tm, final):
    n = x2d.shape[0]
    row = lambda w: pl.BlockSpec((tm, w), lambda i: (i, 0))
    full = lambda a: pl.BlockSpec(a.shape, lambda i: (0,) * a.ndim)
    return pl.pallas_call(
        functools.partial(_merge_kernel, final=final),
        grid=(n // tm,),
        in_specs=[row(SWA_WIDTH), row(MLA_WIDTH), row(D_MODEL), row(D_MODEL), row(D_MODEL),
                  row(PLE_DIM), full(wa), full(wb), full(wout), full(gple), full(wpg),
                  full(wpp), full(gfin)],
        out_specs=row(D_MODEL),
        out_shape=jax.ShapeDtypeStruct((n, D_MODEL), F32),
        compiler_params=pltpu.CompilerParams(dimension_semantics=("arbitrary",),
                                             vmem_limit_bytes=VMEM_LIMIT),
        name="merge",
    )(oa, ob, ma, mb, x2d, p2d, wa, wb, wout, gple, wpg, wpp, gfin)


def _pack_weights(w_in, w_uq, w_ukv):
    d = w_in.shape[0]
    win = w_in
    wuq = w_uq.reshape(d, MLA_Q_LORA, MLA_HEADS, MLA_QK)
    wuq = jnp.pad(wuq, ((0, 0), (0, 0), (0, 0), (0, LANES - MLA_QK))).reshape(d, MLA_Q_LORA, MLA_PAD)
    wukv = w_ukv.reshape(d, MLA_KV_LORA, MLA_HEADS, MLA_NOPE + MLA_V)
    wuk = jnp.pad(wukv[..., :MLA_NOPE], ((0, 0), (0, 0), (0, 0), (0, LANES - MLA_NOPE)))
    wuk = wuk.reshape(d, MLA_KV_LORA, MLA_PAD)
    wv = wukv[..., MLA_NOPE:].reshape(d, MLA_KV_LORA, MLA_HEADS // 2, 2, MLA_V)
    zv = jnp.zeros_like(wv[:, :, :, 0])
    wuv = jnp.stack([jnp.concatenate([wv[:, :, :, 0], zv], axis=-1),
                     jnp.concatenate([zv, wv[:, :, :, 1]], axis=-1)],
                    axis=3).reshape(d, MLA_KV_LORA, MLA_PAD)
    return win.astype(BF16), wuq.astype(BF16), wuk.astype(BF16), wuv.astype(BF16)


def _rope_inv_tile():
    inv = ROPE_THETA ** (-jnp.arange(0, MLA_ROPE, 2, dtype=F32) / MLA_ROPE)
    return jnp.tile(inv, LANES // (MLA_ROPE // 2))[None, :]


def kernel(x, p, positions, g_mix, w_in, sink, g_q, w_uq, g_kv, w_ukv, w_br_a, w_br_b, w_out,
           g_ple, w_ple_gate, w_ple_proj, g_final):
    b, s, _ = x.shape
    depth = w_in.shape[0]
    n = b * s
    tm_in, tm_merge, swa_tile, tq, tk = 512, 512, 512, 512, 512
    assert n % tm_in == 0 and n % tm_merge == 0 and s % swa_tile == 0 and s % tq == 0
    assert tk == tq
    assert tm_in % 64 == 0 and tm_in // 64 <= LANES // (MLA_ROPE // 2)

    win, wuq, wuk, wuv = _pack_weights(w_in, w_uq, w_ukv)
    wa, wb, wout = w_br_a.astype(BF16), w_br_b.astype(BF16), w_out.astype(BF16)
    wpg, wpp = w_ple_gate.astype(BF16), w_ple_proj.astype(BF16)
    inv_tile = _rope_inv_tile()
    pos_in = positions.reshape(n // tm_in, tm_in // LANES, LANES)
    pos4 = positions.reshape(b, s // swa_tile, swa_tile // BLOCK, LANES)
    posh = positions.reshape(b, s // BLOCK, 1, LANES)
    gfin = g_final[None, :]

    x2d = x.reshape(n, D_MODEL)
    for i in range(depth):
        aq, ak, av, ag, bg, ma, mb, q, k, v = _in_proj(
            x2d, pos_in, inv_tile, g_mix[i][None, :], win[i], g_q[i][None, :], wuq[i],
            g_kv[i][None, :], wuk[i], wuv[i], tm=tm_in)
        r3 = lambda t: t.reshape(b, s, t.shape[-1])
        oa = _swa(sink[i], r3(aq), r3(ak), r3(av), r3(ag), pos4, posh, tile=swa_tile)
        ob = _mla(r3(q), r3(k), r3(v), r3(bg), tq=tq, tk=tk)
        x2d = _merge(oa.reshape(n, SWA_WIDTH), ob.reshape(n, MLA_WIDTH), ma, mb, x2d,
                     p[i].reshape(n, PLE_DIM), wa[i], wb[i], wout[i], g_ple[i][None, :], wpg[i],
                     wpp[i], gfin, tm=tm_merge, final=(i == depth - 1))
    return x2d.reshape(b, s, D_MODEL)
```

```python
import functools

import jax
import jax.numpy as jnp
from jax import lax
from jax.experimental import pallas as pl
from jax.experimental.pallas import tpu as pltpu

F32 = jnp.float32
BF16 = jnp.bfloat16

D_MODEL = 1024
PLE_DIM = 256
BLOCK = 128
LANES = 128
EPS = 1e-6
NEG = -1e30
LOG2E = 1.4426950408889634

SWA_HEADS = 8
SWA_KV_HEADS = 2
SWA_HEAD_DIM = 64
SWA_WIDTH = SWA_HEADS * SWA_HEAD_DIM
SWA_KV_WIDTH = SWA_KV_HEADS * SWA_HEAD_DIM

MLA_HEADS = 8
MLA_NOPE = 64
MLA_ROPE = 32
MLA_V = 64
MLA_Q_LORA = 256
MLA_KV_LORA = 128
MLA_WIDTH = MLA_HEADS * MLA_V
MLA_QK = MLA_NOPE + MLA_ROPE
MLA_PAD = MLA_HEADS * LANES
ROPE_THETA = 10000.0

_C_AQ = 0
_C_AK = _C_AQ + SWA_WIDTH
_C_AV = _C_AK + SWA_KV_WIDTH
_C_AG = _C_AV + SWA_KV_WIDTH
_C_QD = _C_AG + SWA_WIDTH
_C_KVD = _C_QD + MLA_Q_LORA
_C_KR = _C_KVD + MLA_KV_LORA
_C_BG = _C_KR + LANES
_C_MA = _C_BG + MLA_WIDTH
_C_MB = _C_MA + D_MODEL
_C_END = _C_MB + D_MODEL

_DIST_MASK = 2.0 ** 8 * 1e30

VMEM_LIMIT = 56 * 1024 * 1024


def _rms(x, g):
    return x * lax.rsqrt(jnp.mean(x * x, axis=-1, keepdims=True) + EPS) * g


def _sigmoid(x):
    return 1.0 / (1.0 + jnp.exp(-x))


def _col_bcast(row):
    return jnp.broadcast_to(row, (LANES, LANES)).T


def _dot(a, b):
    return jnp.dot(a, b, preferred_element_type=F32)


def _dot_nt(a, b):
    return lax.dot_general(a, b, (((1,), (1,)), ((), ())), preferred_element_type=F32)


def _in_proj_kernel(x_ref, pos_ref, inv_ref, gmix_ref, win_ref, gq_ref, wuq_ref, gkv_ref,
                    wuk_ref, wuv_ref,
                    aq_ref, ak_ref, av_ref, ag_ref, bg_ref, ma_ref, mb_ref, q_ref, k_ref, v_ref,
                    wpad_sc, *, tm):
    @pl.when(pl.program_id(0) == 0)
    def _():
        wpad_sc[:, :_C_KR] = win_ref[:, :_C_KR].astype(BF16)
        t = win_ref[:, _C_KR:_C_KR + LANES]
        lane = lax.broadcasted_iota(jnp.int32, t.shape, 1)
        kr_lanes = (lane >= MLA_NOPE) & (lane < MLA_QK)
        wpad_sc[:, _C_KR:_C_BG] = jnp.where(kr_lanes, pltpu.roll(t, MLA_NOPE, 1), 0.0).astype(BF16)
        wpad_sc[:, _C_BG:] = win_ref[:, _C_KR + MLA_ROPE:].astype(BF16)

    h = _rms(x_ref[...], gmix_ref[...]).astype(BF16)

    def proj(lo, hi):
        return _dot(h, wpad_sc[:, lo:hi])

    qd = proj(_C_QD, _C_KVD)
    kvr = proj(_C_KVD, _C_BG)

    half = MLA_ROPE // 2
    grp = 64
    ngrp = tm // grp
    pos = pos_ref[0].astype(F32)
    cols = [_col_bcast(pos[g:g + 1, :]) for g in range(tm // LANES)]
    lane_c = lax.broadcasted_iota(jnp.int32, (grp, LANES), 1)
    posc = jnp.zeros((grp, LANES), F32)
    for g in range(ngrp):
        src = cols[(g * grp) // LANES][(g * grp) % LANES:(g * grp) % LANES + grp, :]
        posc = jnp.where(lane_c // half == g, src, posc)
    ang = posc * inv_ref[...]
    cos_c = jnp.cos(ang)
    sin_c = jnp.sin(ang)
    m_lo = (lane_c >= MLA_NOPE) & (lane_c < MLA_NOPE + half)
    m_hi = (lane_c >= MLA_NOPE + half) & (lane_c < MLA_QK)
    cos_g, slo_g, shi_g = [], [], []
    for g in range(ngrp):
        to_lo = (MLA_NOPE - half * g) % LANES
        to_hi = (MLA_NOPE + half - half * g) % LANES
        c_lo, c_hi = pltpu.roll(cos_c, to_lo, 1), pltpu.roll(cos_c, to_hi, 1)
        s_lo_r, s_hi_r = pltpu.roll(sin_c, to_lo, 1), pltpu.roll(sin_c, to_hi, 1)
        cos_g.append(jnp.where(m_lo, c_lo, jnp.where(m_hi, c_hi, 1.0)))
        slo_g.append(jnp.where(m_lo, -s_lo_r, 0.0))
        shi_g.append(jnp.where(m_hi, s_hi_r, 0.0))
    cos = jnp.concatenate(cos_g, axis=0)
    s_lo = jnp.concatenate(slo_g, axis=0)
    s_hi = jnp.concatenate(shi_g, axis=0)

    def rope(t, c, s1, s2):
        return (t * c + pltpu.roll(t, LANES - half, 1) * s1 + pltpu.roll(t, half, 1) * s2)

    qscale = MLA_QK ** -0.5 * LOG2E
    wide = 2 * LANES

    def akv_piece():
        akv = proj(_C_AK, _C_AG)
        ak_ref[...] = akv[:, :SWA_KV_WIDTH].astype(BF16)
        av_ref[...] = akv[:, SWA_KV_WIDTH:].astype(BF16)

    def piece(ref, col0, j, scale=None):
        def run():
            t = proj(col0 + j * wide, col0 + (j + 1) * wide)
            ref[:, j * wide:(j + 1) * wide] = (t if scale is None else t * scale).astype(BF16)
        return run

    aq_scale = SWA_HEAD_DIM ** -0.5 * LOG2E
    pieces = [akv_piece] + [piece(aq_ref, _C_AQ, j, aq_scale) for j in range(SWA_WIDTH // wide)]
    for ref, col0, width in ((ag_ref, _C_AG, SWA_WIDTH), (bg_ref, _C_BG, MLA_WIDTH),
                             (ma_ref, _C_MA, D_MODEL), (mb_ref, _C_MB, D_MODEL)):
        pieces += [piece(ref, col0, j) for j in range(width // wide)]

    def some_pieces(n):
        for _ in range(min(n, len(pieces))):
            pieces.pop()()

    some_pieces(4)
    qn = _rms(qd, gq_ref[...]).astype(BF16)
    cq, s1q, s2q = cos * qscale, s_lo * qscale, s_hi * qscale
    for pair in range(MLA_HEADS // 2):
        t = _dot(qn, wuq_ref[:, 2 * pair * LANES:(2 * pair + 2) * LANES])
        for hf in range(2):
            c = slice((2 * pair + hf) * LANES, (2 * pair + hf + 1) * LANES)
            q_ref[:, c] = rope(t[:, hf * LANES:(hf + 1) * LANES], cq, s1q, s2q).astype(BF16)
        some_pieces(1)

    kvn = _rms(kvr[:, :MLA_KV_LORA], gkv_ref[...]).astype(BF16)
    kr = rope(kvr[:, MLA_KV_LORA:], cos, s_lo, s_hi)
    some_pieces(1)
    for pair in range(MLA_HEADS // 2):
        t = _dot(kvn, wuk_ref[:, 2 * pair * LANES:(2 * pair + 2) * LANES])
        for hf in range(2):
            c = slice((2 * pair + hf) * LANES, (2 * pair + hf + 1) * LANES)
            k_ref[:, c] = (t[:, hf * LANES:(hf + 1) * LANES] + kr).astype(BF16)
        some_pieces(1)
    for pair in range(MLA_HEADS // 2):
        c = slice(2 * pair * LANES, (2 * pair + 2) * LANES)
        v_ref[:, c] = _dot(kvn, wuv_ref[:, c]).astype(BF16)
        some_pieces(1)
    some_pieces(len(pieces))


def _in_proj(x2d, pos3, inv_tile, gmix, win, gq, wuq, gkv, wuk, wuv, *, tm):
    n = x2d.shape[0]
    grid = (n // tm,)
    row = lambda w: pl.BlockSpec((tm, w), lambda i: (i, 0))
    full = lambda a: pl.BlockSpec(a.shape, lambda i: (0,) * a.ndim)
    widths = (SWA_WIDTH, SWA_KV_WIDTH, SWA_KV_WIDTH, SWA_WIDTH, MLA_WIDTH, D_MODEL, D_MODEL,
              MLA_PAD, MLA_PAD, MLA_PAD)
    return pl.pallas_call(
        functools.partial(_in_proj_kernel, tm=tm),
        grid=grid,
        in_specs=[row(D_MODEL),
                  pl.BlockSpec((1, tm // LANES, LANES), lambda i: (i, 0, 0)),
                  full(inv_tile), full(gmix),
                  pl.BlockSpec(win.shape, lambda i: (0, 0), pipeline_mode=pl.Buffered(1)),
                  full(gq), full(wuq), full(gkv), full(wuk), full(wuv)],
        out_specs=[row(w) for w in widths],
        out_shape=[jax.ShapeDtypeStruct((n, w), BF16) for w in widths],
        scratch_shapes=[pltpu.VMEM((D_MODEL, _C_END), BF16)],
        compiler_params=pltpu.CompilerParams(dimension_semantics=("arbitrary",),
                                             vmem_limit_bytes=VMEM_LIMIT),
        name="in_proj",
    )(x2d, pos3, inv_tile, gmix, win, gq, wuq, gkv, wuk, wuv)


def _swa_kernel(sink_ref, q_ref, k_ref, kh_ref, v_ref, vh_ref, g_ref, pos_ref, posh_ref, o_ref,
                *, tile):
    i = pl.program_id(1)
    nsub = tile // BLOCK
    group = SWA_HEADS // SWA_KV_HEADS

    def variants(main_ref, halo_ref):
        t = jnp.concatenate([halo_ref[0], main_ref[0]], axis=0).astype(F32)
        r = pltpu.roll(t, SWA_HEAD_DIM, 1)
        lo = lax.broadcasted_iota(jnp.int32, t.shape, 1) < SWA_HEAD_DIM
        z = jnp.zeros_like(t)
        return [[jnp.where(lo, t, z).astype(BF16), jnp.where(lo, z, r).astype(BF16)],
                [jnp.where(lo, r, z).astype(BF16), jnp.where(lo, z, t).astype(BF16)]]

    kv = variants(k_ref, kh_ref)
    vv = variants(v_ref, vh_ref)
    pos_rows = jnp.concatenate([posh_ref[0, 0], pos_ref[0, 0]], axis=0)

    ii = lax.broadcasted_iota(jnp.int32, (BLOCK, 2 * BLOCK), 0)
    jj = lax.broadcasted_iota(jnp.int32, (BLOCK, 2 * BLOCK), 1)
    band = (jj > ii) & (jj <= ii + BLOCK)

    for sub in range(nsub):
        rows = slice(sub * BLOCK, (sub + 1) * BLOCK)
        krows = slice(sub * BLOCK, (sub + 2) * BLOCK)
        pq = _col_bcast(pos_rows[sub + 1:sub + 2, :])[:, 0:1]
        pk = jnp.concatenate([pos_rows[sub:sub + 1, :], pos_rows[sub + 1:sub + 2, :]], axis=1)
        dist = (pq - pk).astype(F32)
        valid = band
        if sub == 0:
            valid = band & ((jj >= BLOCK) | (i > 0))
        distm = jnp.where(valid, dist, _DIST_MASK)
        for t in range(SWA_HEADS // 2):
            g = (2 * t) // group
            qp = q_ref[0, rows, t * LANES:(t + 1) * LANES]
            acc = None
            for hf in range(2):
                hd = 2 * t + hf
                slope = 2.0 ** (-8.0 * (hd + 1) / SWA_HEADS) * LOG2E
                sink = sink_ref[hd] * LOG2E
                s = _dot_nt(qp, kv[g][hf][krows]) - slope * distm
                m = jnp.maximum(jnp.max(s, axis=-1, keepdims=True), sink)
                e = jnp.exp2(s - m)
                den = jnp.sum(e, axis=-1, keepdims=True) + jnp.exp2(sink - m)
                pv = _dot(e.astype(BF16), vv[g][hf][krows]) / den
                acc = pv if acc is None else acc + pv
            gate = g_ref[0, rows, t * LANES:(t + 1) * LANES].astype(F32)
            o_ref[0, rows, t * LANES:(t + 1) * LANES] = (acc * gate * _sigmoid(gate)).astype(BF16)


def _swa(sink, q, k, v, gate, pos4, posh, *, tile):
    b, s, _ = q.shape
    nsub = tile // BLOCK
    grid = (b, s // tile)
    main = lambda w: pl.BlockSpec((1, tile, w), lambda bi, i: (bi, i, 0))
    halo = pl.BlockSpec((1, BLOCK, SWA_KV_WIDTH),
                        lambda bi, i: (bi, jnp.maximum(i * nsub - 1, 0), 0))
    return pl.pallas_call(
        functools.partial(_swa_kernel, tile=tile),
        grid=grid,
        in_specs=[pl.BlockSpec(memory_space=pltpu.SMEM),
                  main(SWA_WIDTH), main(SWA_KV_WIDTH), halo, main(SWA_KV_WIDTH), halo,
                  main(SWA_WIDTH),
                  pl.BlockSpec((1, 1, nsub, LANES), lambda bi, i: (bi, i, 0, 0)),
                  pl.BlockSpec((1, 1, 1, LANES),
                               lambda bi, i: (bi, jnp.maximum(i * nsub - 1, 0), 0, 0))],
        out_specs=main(SWA_WIDTH),
        out_shape=jax.ShapeDtypeStruct((b, s, SWA_WIDTH), BF16),
        compiler_params=pltpu.CompilerParams(dimension_semantics=("arbitrary", "arbitrary"),
                                             vmem_limit_bytes=VMEM_LIMIT),
        name="swa",
    )(sink, q, k, k, v, v, gate, pos4, posh)


def _mla_kernel(q_ref, k_ref, v_ref, g_ref, o_ref, m_sc, acc_sc, p_sc, alpha_sc, *, tq, tk):
    qi = pl.program_id(1)
    half = tq // 2
    full = slice(0, tq)
    lo_row = lax.broadcasted_iota(jnp.int32, (1, LANES), 1) < MLA_V

    all_pairs = range(MLA_HEADS // 2)

    def scores(rows, start, width, mask_off, pairs=all_pairs, first=False):
        nr = rows.stop - rows.start
        lo = jnp.broadcast_to(lo_row, (nr, LANES))
        if mask_off is not None:
            row = lax.broadcasted_iota(jnp.int32, (nr, width), 0)
            col = lax.broadcasted_iota(jnp.int32, (nr, width), 1)
            visible = col <= row + mask_off
        for pair in pairs:
            alphas = []
            for hd in (2 * pair, 2 * pair + 1):
                c = slice(hd * LANES, (hd + 1) * LANES)
                s = _dot_nt(q_ref[0, rows, c], k_ref[0, pl.ds(start, width), c])
                if mask_off is not None:
                    s = jnp.where(visible, s, NEG)
                m_cur = jnp.max(s, axis=-1, keepdims=True)
                if first:
                    m_new = jnp.broadcast_to(m_cur, (nr, LANES))
                else:
                    m_prev = m_sc[hd, rows]
                    m_new = jnp.maximum(m_prev, m_cur)
                    alphas.append(jnp.exp2(m_prev - m_new))
                p_sc[hd, rows, 0:width] = jnp.exp2(
                    s - jnp.tile(m_new, (1, width // LANES))).astype(BF16)
                m_sc[hd, rows] = m_new
            if first:
                alpha_sc[pair, rows] = jnp.zeros((nr, LANES), F32)
            else:
                alpha_sc[pair, rows] = jnp.where(lo, alphas[0], alphas[1])

    def weighted(rows, start, width, pairs=all_pairs, first=False):
        ones = [jnp.broadcast_to(jnp.where(lo_row, 1.0, 0.0).astype(BF16), (width, LANES)),
                jnp.broadcast_to(jnp.where(lo_row, 0.0, 1.0).astype(BF16), (width, LANES))]
        for pair in pairs:
            ps, vts = [], []
            for hf in range(2):
                hd = 2 * pair + hf
                c = slice(hd * LANES, (hd + 1) * LANES)
                ps.append(p_sc[hd, rows, 0:width])
                vts.append(jnp.concatenate([v_ref[0, pl.ds(start, width), c], ones[hf]], axis=1))
            pv = _dot(jnp.concatenate(ps, axis=1), jnp.concatenate(vts, axis=0))
            if first:
                acc_sc[pair, rows] = pv
            else:
                acc_sc[pair, rows] = (jnp.tile(alpha_sc[pair, rows], (1, 2)) * acc_sc[pair, rows]
                                      + pv)

    top, bottom = slice(0, half), slice(half, tq)
    diag = pl.multiple_of(qi * tq, tq)

    @pl.when(qi == 0)
    def _():
        scores(top, 0, half, 0, first=True)
        scores(bottom, 0, tq, half, first=True)
        weighted(top, 0, half, first=True)
        weighted(bottom, 0, tq, first=True)

    @pl.when(qi > 0)
    def _():
        acc_sc[...] = jnp.zeros(acc_sc.shape, F32)
        scores(full, 0, tk, None, first=True)

        def body(j, c):
            for pair in all_pairs:
                weighted(full, pl.multiple_of((j - 1) * tk, tk), tk, (pair,))
                scores(full, pl.multiple_of(j * tk, tk), tk, None, (pair,))
            return c

        lax.fori_loop(1, qi, body, 0)
        prev = pl.multiple_of((qi - 1) * tk, tk)
        for pair in all_pairs:
            weighted(full, prev, tk, (pair,))
            scores(top, diag, half, 0, (pair,))
            scores(bottom, diag, tq, half, (pair,))
        weighted(top, diag, half)
        weighted(bottom, diag, tq)

    for pair in range(MLA_HEADS // 2):
        acc = acc_sc[pair]
        vcols = slice(pair * LANES, (pair + 1) * LANES)
        gate = g_ref[0, :, vcols].astype(F32)
        o_ref[0, :, vcols] = (acc[:, :LANES] / acc[:, LANES:] * gate * _sigmoid(gate)).astype(BF16)


def _mla(q, k, v, gate, *, tq, tk):
    b, s, _ = q.shape
    grid = (b, s // tq)
    blk = lambda w: pl.BlockSpec((1, tq, w), lambda bi, i: (bi, i, 0))
    seq = lambda w: pl.BlockSpec((1, s, w), lambda bi, i: (bi, 0, 0))
    return pl.pallas_call(
        functools.partial(_mla_kernel, tq=tq, tk=tk),
        grid=grid,
        in_specs=[blk(MLA_PAD), seq(MLA_PAD), seq(MLA_PAD), blk(MLA_WIDTH)],
        out_specs=blk(MLA_WIDTH),
        out_shape=jax.ShapeDtypeStruct((b, s, MLA_WIDTH), BF16),
        scratch_shapes=[pltpu.VMEM((MLA_HEADS, tq, LANES), F32),
                        pltpu.VMEM((MLA_HEADS // 2, tq, 2 * LANES), F32),
                        pltpu.VMEM((MLA_HEADS, tq, tk), BF16),
                        pltpu.VMEM((MLA_HEADS // 2, tq, LANES), F32)],
        compiler_params=pltpu.CompilerParams(dimension_semantics=("arbitrary", "arbitrary"),
                                             vmem_limit_bytes=VMEM_LIMIT),
        name="mla",
    )(q, k, v, gate)


def _merge_kernel(oa_ref, ob_ref, ma_ref, mb_ref, x_ref, p_ref, wa_ref, wb_ref, wout_ref,
                  gple_ref, wpg_ref, wpp_ref, gfin_ref, out_ref, *, final):
    ya = _dot(oa_ref[...], wa_ref[...])
    yb = _dot(ob_ref[...], wb_ref[...])
    y = _sigmoid(ma_ref[...].astype(F32)) * ya + _sigmoid(mb_ref[...].astype(F32)) * yb
    x1 = x_ref[...] + _dot(y.astype(BF16), wout_ref[...])
    pp = _dot(p_ref[...].astype(BF16), wpp_ref[...])
    hn = _rms(x1, gple_ref[...]).astype(BF16)
    wide = 2 * LANES
    pieces = []
    for j in range(D_MODEL // wide):
        c = slice(j * wide, (j + 1) * wide)
        x2 = x1[:, c] + _sigmoid(_dot(hn, wpg_ref[:, c])) * pp[:, c]
        if final:
            pieces.append(x2)
        else:
            out_ref[:, c] = x2
    if final:
        out_ref[...] = _rms(jnp.concatenate(pieces, axis=1), gfin_ref[...])


def _merge(oa, ob, ma, mb, x2d, p3d, wa, wb, wout, gple, wpg, wpp, gfin, *, tm, layer, final):
    n = x2d.shape[0]
    row = lambda w: pl.BlockSpec((tm, w), lambda i: (i, 0))
    full = lambda a: pl.BlockSpec(a.shape, lambda i: (0,) * a.ndim)
    return pl.pallas_call(
        functools.partial(_merge_kernel, final=final),
        grid=(n // tm,),
        in_specs=[row(SWA_WIDTH), row(MLA_WIDTH), row(D_MODEL), row(D_MODEL), row(D_MODEL),
                  pl.BlockSpec((None, tm, PLE_DIM), lambda i: (layer, i, 0)),
                  full(wa), full(wb), full(wout), full(gple), full(wpg), full(wpp), full(gfin)],
        out_specs=row(D_MODEL),
        out_shape=jax.ShapeDtypeStruct((n, D_MODEL), F32),
        compiler_params=pltpu.CompilerParams(dimension_semantics=("arbitrary",),
                                             vmem_limit_bytes=VMEM_LIMIT),
        name="merge",
    )(oa, ob, ma, mb, x2d, p3d, wa, wb, wout, gple, wpg, wpp, gfin)


def _pack_weights(w_in, w_uq, w_ukv):
    d = w_in.shape[0]
    win = w_in
    wuq = w_uq.reshape(d, MLA_Q_LORA, MLA_HEADS, MLA_QK)
    wuq = jnp.pad(wuq, ((0, 0), (0, 0), (0, 0), (0, LANES - MLA_QK))).reshape(d, MLA_Q_LORA, MLA_PAD)
    wukv = w_ukv.reshape(d, MLA_KV_LORA, MLA_HEADS, MLA_NOPE + MLA_V)
    wuk = jnp.pad(wukv[..., :MLA_NOPE], ((0, 0), (0, 0), (0, 0), (0, LANES - MLA_NOPE)))
    wuk = wuk.reshape(d, MLA_KV_LORA, MLA_PAD)
    wv = wukv[..., MLA_NOPE:].reshape(d, MLA_KV_LORA, MLA_HEADS // 2, 2, MLA_V)
    zv = jnp.zeros_like(wv[:, :, :, 0])
    wuv = jnp.stack([jnp.concatenate([wv[:, :, :, 0], zv], axis=-1),
                     jnp.concatenate([zv, wv[:, :, :, 1]], axis=-1)],
                    axis=3).reshape(d, MLA_KV_LORA, MLA_PAD)
    return win, wuq.astype(BF16), wuk.astype(BF16), wuv.astype(BF16)


def _rope_inv_tile():
    inv = ROPE_THETA ** (-jnp.arange(0, MLA_ROPE, 2, dtype=F32) / MLA_ROPE)
    return jnp.tile(inv, LANES // (MLA_ROPE // 2))[None, :]


def kernel(x, p, positions, g_mix, w_in, sink, g_q, w_uq, g_kv, w_ukv, w_br_a, w_br_b, w_out,
           g_ple, w_ple_gate, w_ple_proj, g_final):
    b, s, _ = x.shape
    depth = w_in.shape[0]
    n = b * s
    tm_in, tm_merge, swa_tile, tq, tk = 512, 512, 512, 512, 512
    assert n % tm_in == 0 and n % tm_merge == 0 and s % swa_tile == 0 and s % tq == 0
    assert tk == tq
    assert tm_in % 64 == 0 and tm_in // 64 <= LANES // (MLA_ROPE // 2)

    win, wuq, wuk, wuv = _pack_weights(w_in, w_uq, w_ukv)
    wa, wb, wout = w_br_a.astype(BF16), w_br_b.astype(BF16), w_out.astype(BF16)
    wpg, wpp = w_ple_gate.astype(BF16), w_ple_proj.astype(BF16)
    inv_tile = _rope_inv_tile()
    pos_in = positions.reshape(n // tm_in, tm_in // LANES, LANES)
    pos4 = positions.reshape(b, s // swa_tile, swa_tile // BLOCK, LANES)
    posh = positions.reshape(b, s // BLOCK, 1, LANES)
    gfin = g_final[None, :]
    p3d = p.reshape(depth, n, PLE_DIM)

    x2d = x.reshape(n, D_MODEL)
    for i in range(depth):
        aq, ak, av, ag, bg, ma, mb, q, k, v = _in_proj(
            x2d, pos_in, inv_tile, g_mix[i][None, :], win[i], g_q[i][None, :], wuq[i],
            g_kv[i][None, :], wuk[i], wuv[i], tm=tm_in)
        r3 = lambda t: t.reshape(b, s, t.shape[-1])
        oa = _swa(sink[i], r3(aq), r3(ak), r3(av), r3(ag), pos4, posh, tile=swa_tile)
        ob = _mla(r3(q), r3(k), r3(v), r3(bg), tq=tq, tk=tk)
        x2d = _merge(oa.reshape(n, SWA_WIDTH), ob.reshape(n, MLA_WIDTH), ma, mb, x2d,
                     p3d, wa[i], wb[i], wout[i], g_ple[i][None, :], wpg[i],
                     wpp[i], gfin, tm=tm_merge, layer=i, final=(i == depth - 1))
    return x2d.reshape(b, s, D_MODEL)
```

```python
import functools

import jax
import jax.numpy as jnp
from jax import lax
from jax.experimental import pallas as pl
from jax.experimental.pallas import tpu as pltpu

F32 = jnp.float32
BF16 = jnp.bfloat16

D_MODEL = 1024
PLE_DIM = 256
BLOCK = 128
LANES = 128
EPS = 1e-6
NEG = -1e30
LOG2E = 1.4426950408889634

SWA_HEADS = 8
SWA_KV_HEADS = 2
SWA_HEAD_DIM = 64
SWA_WIDTH = SWA_HEADS * SWA_HEAD_DIM
SWA_KV_WIDTH = SWA_KV_HEADS * SWA_HEAD_DIM

MLA_HEADS = 8
MLA_NOPE = 64
MLA_ROPE = 32
MLA_V = 64
MLA_Q_LORA = 256
MLA_KV_LORA = 128
MLA_WIDTH = MLA_HEADS * MLA_V
MLA_QK = MLA_NOPE + MLA_ROPE
MLA_PAD = MLA_HEADS * LANES
ROPE_THETA = 10000.0

_C_AQ = 0
_C_AK = _C_AQ + SWA_WIDTH
_C_AV = _C_AK + SWA_KV_WIDTH
_C_AG = _C_AV + SWA_KV_WIDTH
_C_QD = _C_AG + SWA_WIDTH
_C_KVD = _C_QD + MLA_Q_LORA
_C_KR = _C_KVD + MLA_KV_LORA
_C_BG = _C_KR + LANES
_C_MA = _C_BG + MLA_WIDTH
_C_MB = _C_MA + D_MODEL
_C_END = _C_MB + D_MODEL

_DIST_MASK = 2.0 ** 8 * 1e30

VMEM_LIMIT = 56 * 1024 * 1024


def _rms(x, g):
    return x * lax.rsqrt(jnp.mean(x * x, axis=-1, keepdims=True) + EPS) * g


def _sigmoid(x):
    return 1.0 / (1.0 + jnp.exp(-x))


def _col_bcast(row):
    return jnp.broadcast_to(row, (LANES, LANES)).T


def _dot(a, b):
    return jnp.dot(a, b, preferred_element_type=F32)


def _dot_nt(a, b):
    return lax.dot_general(a, b, (((1,), (1,)), ((), ())), preferred_element_type=F32)


def _in_proj_kernel(x_ref, pos_ref, inv_ref, gmix_ref, win_ref, gq_ref, wuq_ref, gkv_ref,
                    wuk_ref, wuv_ref,
                    aq_ref, ak_ref, av_ref, ag_ref, bg_ref, ma_ref, mb_ref, q_ref, k_ref, v_ref,
                    wpad_sc, *, tm):
    @pl.when(pl.program_id(0) == 0)
    def _():
        wpad_sc[:, :_C_KR] = win_ref[:, :_C_KR].astype(BF16)
        t = win_ref[:, _C_KR:_C_KR + LANES]
        lane = lax.broadcasted_iota(jnp.int32, t.shape, 1)
        kr_lanes = (lane >= MLA_NOPE) & (lane < MLA_QK)
        wpad_sc[:, _C_KR:_C_BG] = jnp.where(kr_lanes, pltpu.roll(t, MLA_NOPE, 1), 0.0).astype(BF16)
        wpad_sc[:, _C_BG:] = win_ref[:, _C_KR + MLA_ROPE:].astype(BF16)

    h = _rms(x_ref[...], gmix_ref[...]).astype(BF16)

    def proj(lo, hi):
        return _dot(h, wpad_sc[:, lo:hi])

    qd = proj(_C_QD, _C_KVD)
    kvr = proj(_C_KVD, _C_BG)

    half = MLA_ROPE // 2
    grp = 64
    ngrp = tm // grp
    pos = pos_ref[0].astype(F32)
    cols = [_col_bcast(pos[g:g + 1, :]) for g in range(tm // LANES)]
    lane_c = lax.broadcasted_iota(jnp.int32, (grp, LANES), 1)
    posc = jnp.zeros((grp, LANES), F32)
    for g in range(ngrp):
        src = cols[(g * grp) // LANES][(g * grp) % LANES:(g * grp) % LANES + grp, :]
        posc = jnp.where(lane_c // half == g, src, posc)
    ang = posc * inv_ref[...]
    cos_c = jnp.cos(ang)
    sin_c = jnp.sin(ang)
    m_lo = (lane_c >= MLA_NOPE) & (lane_c < MLA_NOPE + half)
    m_hi = (lane_c >= MLA_NOPE + half) & (lane_c < MLA_QK)
    cos_g, slo_g, shi_g = [], [], []
    for g in range(ngrp):
        to_lo = (MLA_NOPE - half * g) % LANES
        to_hi = (MLA_NOPE + half - half * g) % LANES
        c_lo, c_hi = pltpu.roll(cos_c, to_lo, 1), pltpu.roll(cos_c, to_hi, 1)
        s_lo_r, s_hi_r = pltpu.roll(sin_c, to_lo, 1), pltpu.roll(sin_c, to_hi, 1)
        cos_g.append(jnp.where(m_lo, c_lo, jnp.where(m_hi, c_hi, 1.0)))
        slo_g.append(jnp.where(m_lo, -s_lo_r, 0.0))
        shi_g.append(jnp.where(m_hi, s_hi_r, 0.0))
    cos = jnp.concatenate(cos_g, axis=0)
    s_lo = jnp.concatenate(slo_g, axis=0)
    s_hi = jnp.concatenate(shi_g, axis=0)

    def rope(t, c, s1, s2):
        return (t * c + pltpu.roll(t, LANES - half, 1) * s1 + pltpu.roll(t, half, 1) * s2)

    qscale = MLA_QK ** -0.5 * LOG2E
    wide = 2 * LANES

    def akv_piece():
        akv = proj(_C_AK, _C_AG)
        ak_ref[...] = akv[:, :SWA_KV_WIDTH].astype(BF16)
        av_ref[...] = akv[:, SWA_KV_WIDTH:].astype(BF16)

    def piece(ref, col0, j, scale=None):
        def run():
            t = proj(col0 + j * wide, col0 + (j + 1) * wide)
            ref[:, j * wide:(j + 1) * wide] = (t if scale is None else t * scale).astype(BF16)
        return run

    aq_scale = SWA_HEAD_DIM ** -0.5 * LOG2E
    pieces = [akv_piece] + [piece(aq_ref, _C_AQ, j, aq_scale) for j in range(SWA_WIDTH // wide)]
    for ref, col0, width in ((ag_ref, _C_AG, SWA_WIDTH), (bg_ref, _C_BG, MLA_WIDTH),
                             (ma_ref, _C_MA, D_MODEL), (mb_ref, _C_MB, D_MODEL)):
        pieces += [piece(ref, col0, j) for j in range(width // wide)]

    def some_pieces(n):
        for _ in range(min(n, len(pieces))):
            pieces.pop()()

    some_pieces(4)
    qn = _rms(qd, gq_ref[...]).astype(BF16)
    cq, s1q, s2q = cos * qscale, s_lo * qscale, s_hi * qscale
    for pair in range(MLA_HEADS // 2):
        t = _dot(qn, wuq_ref[:, 2 * pair * LANES:(2 * pair + 2) * LANES])
        for hf in range(2):
            c = slice((2 * pair + hf) * LANES, (2 * pair + hf + 1) * LANES)
            q_ref[:, c] = rope(t[:, hf * LANES:(hf + 1) * LANES], cq, s1q, s2q).astype(BF16)
        some_pieces(1)

    kvn = _rms(kvr[:, :MLA_KV_LORA], gkv_ref[...]).astype(BF16)
    kr = rope(kvr[:, MLA_KV_LORA:], cos, s_lo, s_hi)
    some_pieces(1)
    for pair in range(MLA_HEADS // 2):
        t = _dot(kvn, wuk_ref[:, 2 * pair * LANES:(2 * pair + 2) * LANES])
        for hf in range(2):
            c = slice((2 * pair + hf) * LANES, (2 * pair + hf + 1) * LANES)
            k_ref[:, c] = (t[:, hf * LANES:(hf + 1) * LANES] + kr).astype(BF16)
        some_pieces(1)
    for pair in range(MLA_HEADS // 2):
        c = slice(2 * pair * LANES, (2 * pair + 2) * LANES)
        v_ref[:, c] = _dot(kvn, wuv_ref[:, c]).astype(BF16)
        some_pieces(1)
    some_pieces(len(pieces))


def _in_proj(x2d, pos3, inv_tile, gmix, win, gq, wuq, gkv, wuk, wuv, *, tm, layer):
    n = x2d.shape[0]
    grid = (n // tm,)
    row = lambda w: pl.BlockSpec((tm, w), lambda i: (i, 0))
    full = lambda a: pl.BlockSpec(a.shape, lambda i: (0,) * a.ndim)
    widths = (SWA_WIDTH, SWA_KV_WIDTH, SWA_KV_WIDTH, SWA_WIDTH, MLA_WIDTH, D_MODEL, D_MODEL,
              MLA_PAD, MLA_PAD, MLA_PAD)
    return pl.pallas_call(
        functools.partial(_in_proj_kernel, tm=tm),
        grid=grid,
        in_specs=[row(D_MODEL),
                  pl.BlockSpec((1, tm // LANES, LANES), lambda i: (i, 0, 0)),
                  full(inv_tile), full(gmix),
                  pl.BlockSpec((None,) + win.shape[1:], lambda i: (layer, 0, 0),
                               pipeline_mode=pl.Buffered(1)),
                  full(gq), full(wuq), full(gkv), full(wuk), full(wuv)],
        out_specs=[row(w) for w in widths],
        out_shape=[jax.ShapeDtypeStruct((n, w), BF16) for w in widths],
        scratch_shapes=[pltpu.VMEM((D_MODEL, _C_END), BF16)],
        compiler_params=pltpu.CompilerParams(dimension_semantics=("arbitrary",),
                                             vmem_limit_bytes=VMEM_LIMIT),
        name="in_proj",
    )(x2d, pos3, inv_tile, gmix, win, gq, wuq, gkv, wuk, wuv)


def _swa_kernel(sink_ref, q_ref, k_ref, kh_ref, v_ref, vh_ref, g_ref, pos_ref, posh_ref, o_ref,
                *, tile):
    i = pl.program_id(1)
    nsub = tile // BLOCK
    group = SWA_HEADS // SWA_KV_HEADS

    def variants(main_ref, halo_ref):
        t = jnp.concatenate([halo_ref[0], main_ref[0]], axis=0).astype(F32)
        r = pltpu.roll(t, SWA_HEAD_DIM, 1)
        lo = lax.broadcasted_iota(jnp.int32, t.shape, 1) < SWA_HEAD_DIM
        z = jnp.zeros_like(t)
        return [[jnp.where(lo, t, z).astype(BF16), jnp.where(lo, z, r).astype(BF16)],
                [jnp.where(lo, r, z).astype(BF16), jnp.where(lo, z, t).astype(BF16)]]

    kv = variants(k_ref, kh_ref)
    vv = variants(v_ref, vh_ref)
    pos_rows = jnp.concatenate([posh_ref[0, 0], pos_ref[0, 0]], axis=0)

    ii = lax.broadcasted_iota(jnp.int32, (BLOCK, 2 * BLOCK), 0)
    jj = lax.broadcasted_iota(jnp.int32, (BLOCK, 2 * BLOCK), 1)
    band = (jj > ii) & (jj <= ii + BLOCK)

    for sub in range(nsub):
        rows = slice(sub * BLOCK, (sub + 1) * BLOCK)
        krows = slice(sub * BLOCK, (sub + 2) * BLOCK)
        pq = _col_bcast(pos_rows[sub + 1:sub + 2, :])[:, 0:1]
        pk = jnp.concatenate([pos_rows[sub:sub + 1, :], pos_rows[sub + 1:sub + 2, :]], axis=1)
        dist = (pq - pk).astype(F32)
        valid = band
        if sub == 0:
            valid = band & ((jj >= BLOCK) | (i > 0))
        distm = jnp.where(valid, dist, _DIST_MASK)
        for t in range(SWA_HEADS // 2):
            g = (2 * t) // group
            qp = q_ref[0, rows, t * LANES:(t + 1) * LANES]
            acc = None
            for hf in range(2):
                hd = 2 * t + hf
                slope = 2.0 ** (-8.0 * (hd + 1) / SWA_HEADS) * LOG2E
                sink = sink_ref[hd] * LOG2E
                s = _dot_nt(qp, kv[g][hf][krows]) - slope * distm
                m = jnp.maximum(jnp.max(s, axis=-1, keepdims=True), sink)
                e = jnp.exp2(s - m)
                den = jnp.sum(e, axis=-1, keepdims=True) + jnp.exp2(sink - m)
                pv = _dot(e.astype(BF16), vv[g][hf][krows]) / den
                acc = pv if acc is None else acc + pv
            gate = g_ref[0, rows, t * LANES:(t + 1) * LANES].astype(F32)
            o_ref[0, rows, t * LANES:(t + 1) * LANES] = (acc * gate * _sigmoid(gate)).astype(BF16)


def _swa(sink, q, k, v, gate, pos4, posh, *, tile):
    b, s, _ = q.shape
    nsub = tile // BLOCK
    grid = (b, s // tile)
    main = lambda w: pl.BlockSpec((1, tile, w), lambda bi, i: (bi, i, 0))
    halo = pl.BlockSpec((1, BLOCK, SWA_KV_WIDTH),
                        lambda bi, i: (bi, jnp.maximum(i * nsub - 1, 0), 0))
    return pl.pallas_call(
        functools.partial(_swa_kernel, tile=tile),
        grid=grid,
        in_specs=[pl.BlockSpec(memory_space=pltpu.SMEM),
                  main(SWA_WIDTH), main(SWA_KV_WIDTH), halo, main(SWA_KV_WIDTH), halo,
                  main(SWA_WIDTH),
                  pl.BlockSpec((1, 1, nsub, LANES), lambda bi, i: (bi, i, 0, 0)),
                  pl.BlockSpec((1, 1, 1, LANES),
                               lambda bi, i: (bi, jnp.maximum(i * nsub - 1, 0), 0, 0))],
        out_specs=main(SWA_WIDTH),
        out_shape=jax.ShapeDtypeStruct((b, s, SWA_WIDTH), BF16),
        compiler_params=pltpu.CompilerParams(dimension_semantics=("arbitrary", "arbitrary"),
                                             vmem_limit_bytes=VMEM_LIMIT),
        name="swa",
    )(sink, q, k, k, v, v, gate, pos4, posh)


def _mla_kernel(q_ref, k_ref, v_ref, g_ref, o_ref, m_sc, acc_sc, p_sc, alpha_sc, *, tq, tk):
    qi = pl.program_id(1)
    half = tq // 2
    full = slice(0, tq)
    lo_row = lax.broadcasted_iota(jnp.int32, (1, LANES), 1) < MLA_V

    all_pairs = range(MLA_HEADS // 2)

    def scores(rows, start, width, mask_off, pairs=all_pairs, first=False):
        nr = rows.stop - rows.start
        lo = jnp.broadcast_to(lo_row, (nr, LANES))
        if mask_off is not None:
            row = lax.broadcasted_iota(jnp.int32, (nr, width), 0)
            col = lax.broadcasted_iota(jnp.int32, (nr, width), 1)
            visible = col <= row + mask_off
        for pair in pairs:
            alphas = []
            for hd in (2 * pair, 2 * pair + 1):
                c = slice(hd * LANES, (hd + 1) * LANES)
                s = _dot_nt(q_ref[0, rows, c], k_ref[0, pl.ds(start, width), c])
                if mask_off is not None:
                    s = jnp.where(visible, s, NEG)
                m_cur = jnp.max(s, axis=-1, keepdims=True)
                if first:
                    m_new = jnp.broadcast_to(m_cur, (nr, LANES))
                else:
                    m_prev = m_sc[hd, rows]
                    m_new = jnp.maximum(m_prev, m_cur)
                    alphas.append(jnp.exp2(m_prev - m_new))
                p_sc[hd, rows, 0:width] = jnp.exp2(
                    s - jnp.tile(m_new, (1, width // LANES))).astype(BF16)
                m_sc[hd, rows] = m_new
            if first:
                alpha_sc[pair, rows] = jnp.zeros((nr, LANES), F32)
            else:
                alpha_sc[pair, rows] = jnp.where(lo, alphas[0], alphas[1])

    def weighted(rows, start, width, pairs=all_pairs, first=False):
        ones = [jnp.broadcast_to(jnp.where(lo_row, 1.0, 0.0).astype(BF16), (width, LANES)),
                jnp.broadcast_to(jnp.where(lo_row, 0.0, 1.0).astype(BF16), (width, LANES))]
        for pair in pairs:
            ps, vts = [], []
            for hf in range(2):
                hd = 2 * pair + hf
                c = slice(hd * LANES, (hd + 1) * LANES)
                ps.append(p_sc[hd, rows, 0:width])
                vts.append(jnp.concatenate([v_ref[0, pl.ds(start, width), c], ones[hf]], axis=1))
            pv = _dot(jnp.concatenate(ps, axis=1), jnp.concatenate(vts, axis=0))
            if first:
                acc_sc[pair, rows] = pv
            else:
                acc_sc[pair, rows] = (jnp.tile(alpha_sc[pair, rows], (1, 2)) * acc_sc[pair, rows]
                                      + pv)

    top, bottom = slice(0, half), slice(half, tq)
    diag = pl.multiple_of(qi * tq, tq)

    @pl.when(qi == 0)
    def _():
        scores(top, 0, half, 0, first=True)
        scores(bottom, 0, tq, half, first=True)
        weighted(top, 0, half, first=True)
        weighted(bottom, 0, tq, first=True)

    @pl.when(qi > 0)
    def _():
        acc_sc[...] = jnp.zeros(acc_sc.shape, F32)
        scores(full, 0, tk, None, first=True)

        def body(j, c):
            for pair in all_pairs:
                weighted(full, pl.multiple_of((j - 1) * tk, tk), tk, (pair,))
                scores(full, pl.multiple_of(j * tk, tk), tk, None, (pair,))
            return c

        lax.fori_loop(1, qi, body, 0)
        prev = pl.multiple_of((qi - 1) * tk, tk)
        for pair in all_pairs:
            weighted(full, prev, tk, (pair,))
            scores(top, diag, half, 0, (pair,))
            scores(bottom, diag, tq, half, (pair,))
        weighted(top, diag, half)
        weighted(bottom, diag, tq)

    for pair in range(MLA_HEADS // 2):
        acc = acc_sc[pair]
        vcols = slice(pair * LANES, (pair + 1) * LANES)
        gate = g_ref[0, :, vcols].astype(F32)
        o_ref[0, :, vcols] = (acc[:, :LANES] / acc[:, LANES:] * gate * _sigmoid(gate)).astype(BF16)


def _mla(q, k, v, gate, *, tq, tk):
    b, s, _ = q.shape
    grid = (b, s // tq)
    blk = lambda w: pl.BlockSpec((1, tq, w), lambda bi, i: (bi, i, 0))
    seq = lambda w: pl.BlockSpec((1, s, w), lambda bi, i: (bi, 0, 0))
    return pl.pallas_call(
        functools.partial(_mla_kernel, tq=tq, tk=tk),
        grid=grid,
        in_specs=[blk(MLA_PAD), seq(MLA_PAD), seq(MLA_PAD), blk(MLA_WIDTH)],
        out_specs=blk(MLA_WIDTH),
        out_shape=jax.ShapeDtypeStruct((b, s, MLA_WIDTH), BF16),
        scratch_shapes=[pltpu.VMEM((MLA_HEADS, tq, LANES), F32),
                        pltpu.VMEM((MLA_HEADS // 2, tq, 2 * LANES), F32),
                        pltpu.VMEM((MLA_HEADS, tq, tk), BF16),
                        pltpu.VMEM((MLA_HEADS // 2, tq, LANES), F32)],
        compiler_params=pltpu.CompilerParams(dimension_semantics=("arbitrary", "arbitrary"),
                                             vmem_limit_bytes=VMEM_LIMIT),
        name="mla",
    )(q, k, v, gate)


def _merge_kernel(oa_ref, ob_ref, ma_ref, mb_ref, x_ref, p_ref, wa32_ref, wb32_ref, wout32_ref,
                  gple_ref, wpg32_ref, wpp32_ref, gfin_ref, out_ref,
                  wa_ref, wb_ref, wout_ref, wpg_ref, wpp_ref, *, final):
    @pl.when(pl.program_id(0) == 0)
    def _():
        for src, dst in ((wa32_ref, wa_ref), (wb32_ref, wb_ref), (wout32_ref, wout_ref),
                         (wpg32_ref, wpg_ref), (wpp32_ref, wpp_ref)):
            dst[...] = src[...].astype(BF16)

    ya = _dot(oa_ref[...], wa_ref[...])
    yb = _dot(ob_ref[...], wb_ref[...])
    y = _sigmoid(ma_ref[...].astype(F32)) * ya + _sigmoid(mb_ref[...].astype(F32)) * yb
    x1 = x_ref[...] + _dot(y.astype(BF16), wout_ref[...])
    pp = _dot(p_ref[...].astype(BF16), wpp_ref[...])
    hn = _rms(x1, gple_ref[...]).astype(BF16)
    wide = 2 * LANES
    pieces = []
    for j in range(D_MODEL // wide):
        c = slice(j * wide, (j + 1) * wide)
        x2 = x1[:, c] + _sigmoid(_dot(hn, wpg_ref[:, c])) * pp[:, c]
        if final:
            pieces.append(x2)
        else:
            out_ref[:, c] = x2
    if final:
        out_ref[...] = _rms(jnp.concatenate(pieces, axis=1), gfin_ref[...])


def _merge(oa, ob, ma, mb, x2d, p3d, wa, wb, wout, gple, wpg, wpp, gfin, *, tm, layer, final):
    n = x2d.shape[0]
    row = lambda w: pl.BlockSpec((tm, w), lambda i: (i, 0))
    full = lambda a: pl.BlockSpec(a.shape, lambda i: (0,) * a.ndim)
    slab = lambda a: pl.BlockSpec((None,) + a.shape[1:], lambda i: (layer, 0, 0),
                                  pipeline_mode=pl.Buffered(1))
    weights = (wa, wb, wout, wpg, wpp)
    return pl.pallas_call(
        functools.partial(_merge_kernel, final=final),
        grid=(n // tm,),
        in_specs=[row(SWA_WIDTH), row(MLA_WIDTH), row(D_MODEL), row(D_MODEL), row(D_MODEL),
                  pl.BlockSpec((None, tm, PLE_DIM), lambda i: (layer, i, 0)),
                  slab(wa), slab(wb), slab(wout), full(gple), slab(wpg), slab(wpp), full(gfin)],
        out_specs=row(D_MODEL),
        out_shape=jax.ShapeDtypeStruct((n, D_MODEL), F32),
        scratch_shapes=[pltpu.VMEM(w.shape[1:], BF16) for w in weights],
        compiler_params=pltpu.CompilerParams(dimension_semantics=("arbitrary",),
                                             vmem_limit_bytes=VMEM_LIMIT),
        name="merge",
    )(oa, ob, ma, mb, x2d, p3d, wa, wb, wout, gple, wpg, wpp, gfin)


def _pack_weights(w_in, w_uq, w_ukv):
    d = w_in.shape[0]
    win = w_in
    wuq = w_uq.reshape(d, MLA_Q_LORA, MLA_HEADS, MLA_QK)
    wuq = jnp.pad(wuq, ((0, 0), (0, 0), (0, 0), (0, LANES - MLA_QK))).reshape(d, MLA_Q_LORA, MLA_PAD)
    wukv = w_ukv.reshape(d, MLA_KV_LORA, MLA_HEADS, MLA_NOPE + MLA_V)
    wuk = jnp.pad(wukv[..., :MLA_NOPE], ((0, 0), (0, 0), (0, 0), (0, LANES - MLA_NOPE)))
    wuk = wuk.reshape(d, MLA_KV_LORA, MLA_PAD)
    wv = wukv[..., MLA_NOPE:].reshape(d, MLA_KV_LORA, MLA_HEADS // 2, 2, MLA_V)
    zv = jnp.zeros_like(wv[:, :, :, 0])
    wuv = jnp.stack([jnp.concatenate([wv[:, :, :, 0], zv], axis=-1),
                     jnp.concatenate([zv, wv[:, :, :, 1]], axis=-1)],
                    axis=3).reshape(d, MLA_KV_LORA, MLA_PAD)
    return win, wuq.astype(BF16), wuk.astype(BF16), wuv.astype(BF16)


def _rope_inv_tile():
    inv = ROPE_THETA ** (-jnp.arange(0, MLA_ROPE, 2, dtype=F32) / MLA_ROPE)
    return jnp.tile(inv, LANES // (MLA_ROPE // 2))[None, :]


def kernel(x, p, positions, g_mix, w_in, sink, g_q, w_uq, g_kv, w_ukv, w_br_a, w_br_b, w_out,
           g_ple, w_ple_gate, w_ple_proj, g_final):
    b, s, _ = x.shape
    depth = w_in.shape[0]
    n = b * s
    tm_in, tm_merge, swa_tile, tq, tk = 512, 512, 512, 512, 512
    assert n % tm_in == 0 and n % tm_merge == 0 and s % swa_tile == 0 and s % tq == 0
    assert tk == tq
    assert tm_in % 64 == 0 and tm_in // 64 <= LANES // (MLA_ROPE // 2)

    win, wuq, wuk, wuv = _pack_weights(w_in, w_uq, w_ukv)
    inv_tile = _rope_inv_tile()
    pos_in = positions.reshape(n // tm_in, tm_in // LANES, LANES)
    pos4 = positions.reshape(b, s // swa_tile, swa_tile // BLOCK, LANES)
    posh = positions.reshape(b, s // BLOCK, 1, LANES)
    gfin = g_final[None, :]
    p3d = p.reshape(depth, n, PLE_DIM)

    x2d = x.reshape(n, D_MODEL)
    for i in range(depth):
        aq, ak, av, ag, bg, ma, mb, q, k, v = _in_proj(
            x2d, pos_in, inv_tile, g_mix[i][None, :], win, g_q[i][None, :], wuq[i],
            g_kv[i][None, :], wuk[i], wuv[i], tm=tm_in, layer=i)
        r3 = lambda t: t.reshape(b, s, t.shape[-1])
        oa = _swa(sink[i], r3(aq), r3(ak), r3(av), r3(ag), pos4, posh, tile=swa_tile)
        ob = _mla(r3(q), r3(k), r3(v), r3(bg), tq=tq, tk=tk)
        x2d = _merge(oa.reshape(n, SWA_WIDTH), ob.reshape(n, MLA_WIDTH), ma, mb, x2d,
                     p3d, w_br_a, w_br_b, w_out, g_ple[i][None, :], w_ple_gate,
                     w_ple_proj, gfin, tm=tm_merge, layer=i, final=(i == depth - 1))
    return x2d.reshape(b, s, D_MODEL)
```

```python
import functools

import jax
import jax.numpy as jnp
from jax import lax
from jax.experimental import pallas as pl
from jax.experimental.pallas import tpu as pltpu

F32 = jnp.float32
BF16 = jnp.bfloat16

D_MODEL = 1024
PLE_DIM = 256
BLOCK = 128
LANES = 128
EPS = 1e-6
NEG = -1e30
LOG2E = 1.4426950408889634

SWA_HEADS = 8
SWA_KV_HEADS = 2
SWA_HEAD_DIM = 64
SWA_WIDTH = SWA_HEADS * SWA_HEAD_DIM
SWA_KV_WIDTH = SWA_KV_HEADS * SWA_HEAD_DIM

MLA_HEADS = 8
MLA_NOPE = 64
MLA_ROPE = 32
MLA_V = 64
MLA_Q_LORA = 256
MLA_KV_LORA = 128
MLA_WIDTH = MLA_HEADS * MLA_V
MLA_QK = MLA_NOPE + MLA_ROPE
MLA_PAD = MLA_HEADS * LANES
ROPE_THETA = 10000.0

_C_AQ = 0
_C_AK = _C_AQ + SWA_WIDTH
_C_AV = _C_AK + SWA_KV_WIDTH
_C_AG = _C_AV + SWA_KV_WIDTH
_C_QD = _C_AG + SWA_WIDTH
_C_KVD = _C_QD + MLA_Q_LORA
_C_KR = _C_KVD + MLA_KV_LORA
_C_BG = _C_KR + LANES
_C_MA = _C_BG + MLA_WIDTH
_C_MB = _C_MA + D_MODEL
_C_END = _C_MB + D_MODEL

_DIST_MASK = 2.0 ** 8 * 1e30

VMEM_LIMIT = 56 * 1024 * 1024


def _rms(x, g):
    return x * lax.rsqrt(jnp.mean(x * x, axis=-1, keepdims=True) + EPS) * g


def _sigmoid(x):
    return 1.0 / (1.0 + jnp.exp(-x))


def _col_bcast(row):
    return jnp.broadcast_to(row, (LANES, LANES)).T


def _dot(a, b):
    return jnp.dot(a, b, preferred_element_type=F32)


def _dot_nt(a, b):
    return lax.dot_general(a, b, (((1,), (1,)), ((), ())), preferred_element_type=F32)


def _in_proj_kernel(x_ref, pos_ref, inv_ref, gmix_ref, win_ref, gq_ref, wuq_ref, gkv_ref,
                    wuk_ref, wuv_ref,
                    aq_ref, ak_ref, av_ref, ag_ref, bg_ref, ma_ref, mb_ref, q_ref, k_ref, v_ref,
                    wpad_sc, *, tm):
    @pl.when(pl.program_id(0) == 0)
    def _():
        wpad_sc[:, :_C_KR] = win_ref[:, :_C_KR].astype(BF16)
        t = win_ref[:, _C_KR:_C_KR + LANES]
        lane = lax.broadcasted_iota(jnp.int32, t.shape, 1)
        kr_lanes = (lane >= MLA_NOPE) & (lane < MLA_QK)
        wpad_sc[:, _C_KR:_C_BG] = jnp.where(kr_lanes, pltpu.roll(t, MLA_NOPE, 1), 0.0).astype(BF16)
        wpad_sc[:, _C_BG:] = win_ref[:, _C_KR + MLA_ROPE:].astype(BF16)

    h = _rms(x_ref[...], gmix_ref[...]).astype(BF16)

    def proj(lo, hi):
        return _dot(h, wpad_sc[:, lo:hi])

    qd = proj(_C_QD, _C_KVD)
    kvr = proj(_C_KVD, _C_BG)

    half = MLA_ROPE // 2
    grp = 64
    ngrp = tm // grp
    pos = pos_ref[0].astype(F32)
    cols = [_col_bcast(pos[g:g + 1, :]) for g in range(tm // LANES)]
    lane_c = lax.broadcasted_iota(jnp.int32, (grp, LANES), 1)
    posc = jnp.zeros((grp, LANES), F32)
    for g in range(ngrp):
        src = cols[(g * grp) // LANES][(g * grp) % LANES:(g * grp) % LANES + grp, :]
        posc = jnp.where(lane_c // half == g, src, posc)
    ang = posc * inv_ref[...]
    cos_c = jnp.cos(ang)
    sin_c = jnp.sin(ang)
    m_lo = (lane_c >= MLA_NOPE) & (lane_c < MLA_NOPE + half)
    m_hi = (lane_c >= MLA_NOPE + half) & (lane_c < MLA_QK)
    cos_g, slo_g, shi_g = [], [], []
    for g in range(ngrp):
        to_lo = (MLA_NOPE - half * g) % LANES
        to_hi = (MLA_NOPE + half - half * g) % LANES
        c_lo, c_hi = pltpu.roll(cos_c, to_lo, 1), pltpu.roll(cos_c, to_hi, 1)
        s_lo_r, s_hi_r = pltpu.roll(sin_c, to_lo, 1), pltpu.roll(sin_c, to_hi, 1)
        cos_g.append(jnp.where(m_lo, c_lo, jnp.where(m_hi, c_hi, 1.0)))
        slo_g.append(jnp.where(m_lo, -s_lo_r, 0.0))
        shi_g.append(jnp.where(m_hi, s_hi_r, 0.0))
    cos = jnp.concatenate(cos_g, axis=0)
    s_lo = jnp.concatenate(slo_g, axis=0)
    s_hi = jnp.concatenate(shi_g, axis=0)

    lane_t = lax.broadcasted_iota(jnp.int32, (tm, LANES), 1)

    def rope(t, c, s1, s2):
        return (t * c + pltpu.roll(t, LANES - half, 1) * s1 + pltpu.roll(t, half, 1) * s2)

    qscale = MLA_QK ** -0.5 * LOG2E
    wide = 2 * LANES

    def akv_piece():
        akv = proj(_C_AK, _C_AG)
        ak_ref[...] = akv[:, :SWA_KV_WIDTH].astype(BF16)
        av_ref[...] = akv[:, SWA_KV_WIDTH:].astype(BF16)

    def piece(ref, col0, j, scale=None):
        def run():
            t = proj(col0 + j * wide, col0 + (j + 1) * wide)
            ref[:, j * wide:(j + 1) * wide] = (t if scale is None else t * scale).astype(BF16)
        return run

    aq_scale = SWA_HEAD_DIM ** -0.5 * LOG2E
    pieces = [akv_piece] + [piece(aq_ref, _C_AQ, j, aq_scale) for j in range(SWA_WIDTH // wide)]
    for ref, col0, width in ((ag_ref, _C_AG, SWA_WIDTH), (bg_ref, _C_BG, MLA_WIDTH),
                             (ma_ref, _C_MA, D_MODEL), (mb_ref, _C_MB, D_MODEL)):
        pieces += [piece(ref, col0, j) for j in range(width // wide)]

    def some_pieces(n):
        for _ in range(min(n, len(pieces))):
            pieces.pop()()

    some_pieces(4)
    qn = _rms(qd, gq_ref[...]).astype(BF16)
    cq, s1q, s2q = cos * qscale, s_lo * qscale, s_hi * qscale
    for pair in range(MLA_HEADS // 2):
        t = _dot(qn, wuq_ref[:, 2 * pair * LANES:(2 * pair + 2) * LANES])
        for hf in range(2):
            c = slice((2 * pair + hf) * LANES, (2 * pair + hf + 1) * LANES)
            q_ref[:, c] = rope(t[:, hf * LANES:(hf + 1) * LANES], cq, s1q, s2q).astype(BF16)
        some_pieces(1)

    kvn = _rms(kvr[:, :MLA_KV_LORA], gkv_ref[...]).astype(BF16)
    kr = rope(kvr[:, MLA_KV_LORA:], cos, s_lo, s_hi)
    some_pieces(1)
    lo_k = lane_t < MLA_NOPE
    for two_pairs in range(MLA_HEADS // 4):
        t = _dot(kvn, wuk_ref[:, two_pairs * wide:(two_pairs + 1) * wide])
        for j in range(2):
            pair = 2 * two_pairs + j
            pt = t[:, j * LANES:(j + 1) * LANES]
            k_ref[:, 2 * pair * LANES:(2 * pair + 1) * LANES] = (
                jnp.where(lo_k, pt, 0.0) + kr).astype(BF16)
            k_ref[:, (2 * pair + 1) * LANES:(2 * pair + 2) * LANES] = (
                jnp.where(lo_k, pltpu.roll(pt, MLA_NOPE, 1), 0.0) + kr).astype(BF16)
            some_pieces(1)
    lo_v = lane_t < MLA_V
    for two_pairs in range(MLA_HEADS // 4):
        t = _dot(kvn, wuv_ref[:, two_pairs * wide:(two_pairs + 1) * wide])
        for j in range(2):
            pair = 2 * two_pairs + j
            pt = t[:, j * LANES:(j + 1) * LANES]
            v_ref[:, 2 * pair * LANES:(2 * pair + 1) * LANES] = jnp.where(lo_v, pt, 0.0).astype(BF16)
            v_ref[:, (2 * pair + 1) * LANES:(2 * pair + 2) * LANES] = (
                jnp.where(lo_v, 0.0, pt).astype(BF16))
            some_pieces(1)
    some_pieces(len(pieces))


def _in_proj(x2d, pos3, inv_tile, gmix, win, gq, wuq, gkv, wuk, wuv, *, tm, layer):
    n = x2d.shape[0]
    grid = (n // tm,)
    row = lambda w: pl.BlockSpec((tm, w), lambda i: (i, 0))
    full = lambda a: pl.BlockSpec(a.shape, lambda i: (0,) * a.ndim)
    widths = (SWA_WIDTH, SWA_KV_WIDTH, SWA_KV_WIDTH, SWA_WIDTH, MLA_WIDTH, D_MODEL, D_MODEL,
              MLA_PAD, MLA_PAD, MLA_PAD)
    return pl.pallas_call(
        functools.partial(_in_proj_kernel, tm=tm),
        grid=grid,
        in_specs=[row(D_MODEL),
                  pl.BlockSpec((1, tm // LANES, LANES), lambda i: (i, 0, 0)),
                  full(inv_tile), full(gmix),
                  pl.BlockSpec((None,) + win.shape[1:], lambda i: (layer, 0, 0),
                               pipeline_mode=pl.Buffered(1)),
                  full(gq), full(wuq), full(gkv), full(wuk), full(wuv)],
        out_specs=[row(w) for w in widths],
        out_shape=[jax.ShapeDtypeStruct((n, w), BF16) for w in widths],
        scratch_shapes=[pltpu.VMEM((D_MODEL, _C_END), BF16)],
        compiler_params=pltpu.CompilerParams(dimension_semantics=("arbitrary",),
                                             vmem_limit_bytes=VMEM_LIMIT),
        name="in_proj",
    )(x2d, pos3, inv_tile, gmix, win, gq, wuq, gkv, wuk, wuv)


def _swa_kernel(sink_ref, q_ref, k_ref, kh_ref, v_ref, vh_ref, g_ref, pos_ref, posh_ref, o_ref,
                *, tile):
    i = pl.program_id(1)
    nsub = tile // BLOCK
    group = SWA_HEADS // SWA_KV_HEADS

    def variants(main_ref, halo_ref):
        t = jnp.concatenate([halo_ref[0], main_ref[0]], axis=0).astype(F32)
        r = pltpu.roll(t, SWA_HEAD_DIM, 1)
        lo = lax.broadcasted_iota(jnp.int32, t.shape, 1) < SWA_HEAD_DIM
        z = jnp.zeros_like(t)
        return [[jnp.where(lo, t, z).astype(BF16), jnp.where(lo, z, r).astype(BF16)],
                [jnp.where(lo, r, z).astype(BF16), jnp.where(lo, z, t).astype(BF16)]]

    kv = variants(k_ref, kh_ref)
    vv = variants(v_ref, vh_ref)
    pos_rows = jnp.concatenate([posh_ref[0, 0], pos_ref[0, 0]], axis=0)

    ii = lax.broadcasted_iota(jnp.int32, (BLOCK, 2 * BLOCK), 0)
    jj = lax.broadcasted_iota(jnp.int32, (BLOCK, 2 * BLOCK), 1)
    band = (jj > ii) & (jj <= ii + BLOCK)

    for sub in range(nsub):
        rows = slice(sub * BLOCK, (sub + 1) * BLOCK)
        krows = slice(sub * BLOCK, (sub + 2) * BLOCK)
        pq = _col_bcast(pos_rows[sub + 1:sub + 2, :])[:, 0:1]
        pk = jnp.concatenate([pos_rows[sub:sub + 1, :], pos_rows[sub + 1:sub + 2, :]], axis=1)
        dist = (pq - pk).astype(F32)
        valid = band
        if sub == 0:
            valid = band & ((jj >= BLOCK) | (i > 0))
        distm = jnp.where(valid, dist, _DIST_MASK)
        for t in range(SWA_HEADS // 2):
            g = (2 * t) // group
            qp = q_ref[0, rows, t * LANES:(t + 1) * LANES]
            acc = None
            for hf in range(2):
                hd = 2 * t + hf
                slope = 2.0 ** (-8.0 * (hd + 1) / SWA_HEADS) * LOG2E
                sink = sink_ref[hd] * LOG2E
                s = _dot_nt(qp, kv[g][hf][krows]) - slope * distm
                m = jnp.maximum(jnp.max(s, axis=-1, keepdims=True), sink)
                e = jnp.exp2(s - m)
                den = jnp.sum(e, axis=-1, keepdims=True) + jnp.exp2(sink - m)
                pv = _dot(e.astype(BF16), vv[g][hf][krows]) / den
                acc = pv if acc is None else acc + pv
            gate = g_ref[0, rows, t * LANES:(t + 1) * LANES].astype(F32)
            o_ref[0, rows, t * LANES:(t + 1) * LANES] = (acc * gate * _sigmoid(gate)).astype(BF16)


def _swa(sink, q, k, v, gate, pos4, posh, *, tile):
    b, s, _ = q.shape
    nsub = tile // BLOCK
    grid = (b, s // tile)
    main = lambda w: pl.BlockSpec((1, tile, w), lambda bi, i: (bi, i, 0))
    halo = pl.BlockSpec((1, BLOCK, SWA_KV_WIDTH),
                        lambda bi, i: (bi, jnp.maximum(i * nsub - 1, 0), 0))
    return pl.pallas_call(
        functools.partial(_swa_kernel, tile=tile),
        grid=grid,
        in_specs=[pl.BlockSpec(memory_space=pltpu.SMEM),
                  main(SWA_WIDTH), main(SWA_KV_WIDTH), halo, main(SWA_KV_WIDTH), halo,
                  main(SWA_WIDTH),
                  pl.BlockSpec((1, 1, nsub, LANES), lambda bi, i: (bi, i, 0, 0)),
                  pl.BlockSpec((1, 1, 1, LANES),
                               lambda bi, i: (bi, jnp.maximum(i * nsub - 1, 0), 0, 0))],
        out_specs=main(SWA_WIDTH),
        out_shape=jax.ShapeDtypeStruct((b, s, SWA_WIDTH), BF16),
        compiler_params=pltpu.CompilerParams(dimension_semantics=("arbitrary", "arbitrary"),
                                             vmem_limit_bytes=VMEM_LIMIT),
        name="swa",
    )(sink, q, k, k, v, v, gate, pos4, posh)


def _mla_kernel(q_ref, k_ref, v_ref, g_ref, o_ref, m_sc, acc_sc, p_sc, alpha_sc, *, tq, tk):
    qi = pl.program_id(1)
    half = tq // 2
    full = slice(0, tq)
    lo_row = lax.broadcasted_iota(jnp.int32, (1, LANES), 1) < MLA_V

    all_pairs = range(MLA_HEADS // 2)

    def scores(rows, start, width, mask_off, pairs=all_pairs, first=False):
        nr = rows.stop - rows.start
        lo = jnp.broadcast_to(lo_row, (nr, LANES))
        if mask_off is not None:
            row = lax.broadcasted_iota(jnp.int32, (nr, width), 0)
            col = lax.broadcasted_iota(jnp.int32, (nr, width), 1)
            visible = col <= row + mask_off
        for pair in pairs:
            alphas = []
            for hd in (2 * pair, 2 * pair + 1):
                c = slice(hd * LANES, (hd + 1) * LANES)
                s = _dot_nt(q_ref[0, rows, c], k_ref[0, pl.ds(start, width), c])
                if mask_off is not None:
                    s = jnp.where(visible, s, NEG)
                m_cur = jnp.max(s, axis=-1, keepdims=True)
                if first:
                    m_new = jnp.broadcast_to(m_cur, (nr, LANES))
                else:
                    m_prev = m_sc[hd, rows]
                    m_new = jnp.maximum(m_prev, m_cur)
                    alphas.append(jnp.exp2(m_prev - m_new))
                p_sc[hd, rows, 0:width] = jnp.exp2(
                    s - jnp.tile(m_new, (1, width // LANES))).astype(BF16)
                m_sc[hd, rows] = m_new
            if first:
                alpha_sc[pair, rows] = jnp.zeros((nr, LANES), F32)
            else:
                alpha_sc[pair, rows] = jnp.where(lo, alphas[0], alphas[1])

    def weighted(rows, start, width, pairs=all_pairs, first=False):
        ones = [jnp.broadcast_to(jnp.where(lo_row, 1.0, 0.0).astype(BF16), (width, LANES)),
                jnp.broadcast_to(jnp.where(lo_row, 0.0, 1.0).astype(BF16), (width, LANES))]
        for pair in pairs:
            ps, vts = [], []
            for hf in range(2):
                hd = 2 * pair + hf
                c = slice(hd * LANES, (hd + 1) * LANES)
                ps.append(p_sc[hd, rows, 0:width])
                vts.append(jnp.concatenate([v_ref[0, pl.ds(start, width), c], ones[hf]], axis=1))
            pv = _dot(jnp.concatenate(ps, axis=1), jnp.concatenate(vts, axis=0))
            if first:
                acc_sc[pair, rows] = pv
            else:
                acc_sc[pair, rows] = (jnp.tile(alpha_sc[pair, rows], (1, 2)) * acc_sc[pair, rows]
                                      + pv)

    top, bottom = slice(0, half), slice(half, tq)
    diag = pl.multiple_of(qi * tq, tq)

    @pl.when(qi == 0)
    def _():
        scores(top, 0, half, 0, first=True)
        scores(bottom, 0, tq, half, first=True)
        weighted(top, 0, half, first=True)
        weighted(bottom, 0, tq, first=True)

    @pl.when(qi > 0)
    def _():
        acc_sc[...] = jnp.zeros(acc_sc.shape, F32)
        scores(full, 0, tk, None, first=True)

        def body(j, c):
            for pair in all_pairs:
                weighted(full, pl.multiple_of((j - 1) * tk, tk), tk, (pair,))
                scores(full, pl.multiple_of(j * tk, tk), tk, None, (pair,))
            return c

        lax.fori_loop(1, qi, body, 0)
        prev = pl.multiple_of((qi - 1) * tk, tk)
        for pair in all_pairs:
            weighted(full, prev, tk, (pair,))
            scores(top, diag, half, 0, (pair,))
            scores(bottom, diag, tq, half, (pair,))
        weighted(top, diag, half)
        weighted(bottom, diag, tq)

    for pair in range(MLA_HEADS // 2):
        acc = acc_sc[pair]
        vcols = slice(pair * LANES, (pair + 1) * LANES)
        gate = g_ref[0, :, vcols].astype(F32)
        o_ref[0, :, vcols] = (acc[:, :LANES] / acc[:, LANES:] * gate * _sigmoid(gate)).astype(BF16)


def _mla(q, k, v, gate, *, tq, tk):
    b, s, _ = q.shape
    grid = (b, s // tq)
    blk = lambda w: pl.BlockSpec((1, tq, w), lambda bi, i: (bi, i, 0))
    seq = lambda w: pl.BlockSpec((1, s, w), lambda bi, i: (bi, 0, 0))
    return pl.pallas_call(
        functools.partial(_mla_kernel, tq=tq, tk=tk),
        grid=grid,
        in_specs=[blk(MLA_PAD), seq(MLA_PAD), seq(MLA_PAD), blk(MLA_WIDTH)],
        out_specs=blk(MLA_WIDTH),
        out_shape=jax.ShapeDtypeStruct((b, s, MLA_WIDTH), BF16),
        scratch_shapes=[pltpu.VMEM((MLA_HEADS, tq, LANES), F32),
                        pltpu.VMEM((MLA_HEADS // 2, tq, 2 * LANES), F32),
                        pltpu.VMEM((MLA_HEADS, tq, tk), BF16),
                        pltpu.VMEM((MLA_HEADS // 2, tq, LANES), F32)],
        compiler_params=pltpu.CompilerParams(dimension_semantics=("arbitrary", "arbitrary"),
                                             vmem_limit_bytes=VMEM_LIMIT),
        name="mla",
    )(q, k, v, gate)


def _merge_kernel(oa_ref, ob_ref, ma_ref, mb_ref, x_ref, p_ref, wa32_ref, wb32_ref, wout32_ref,
                  gple_ref, wpg32_ref, wpp32_ref, gfin_ref, out_ref,
                  wa_ref, wb_ref, wout_ref, wpg_ref, wpp_ref, *, final):
    @pl.when(pl.program_id(0) == 0)
    def _():
        for src, dst in ((wa32_ref, wa_ref), (wb32_ref, wb_ref), (wout32_ref, wout_ref),
                         (wpg32_ref, wpg_ref), (wpp32_ref, wpp_ref)):
            dst[...] = src[...].astype(BF16)

    ya = _dot(oa_ref[...], wa_ref[...])
    yb = _dot(ob_ref[...], wb_ref[...])
    y = _sigmoid(ma_ref[...].astype(F32)) * ya + _sigmoid(mb_ref[...].astype(F32)) * yb
    x1 = x_ref[...] + _dot(y.astype(BF16), wout_ref[...])
    pp = _dot(p_ref[...].astype(BF16), wpp_ref[...])
    hn = _rms(x1, gple_ref[...]).astype(BF16)
    wide = 2 * LANES
    pieces = []
    for j in range(D_MODEL // wide):
        c = slice(j * wide, (j + 1) * wide)
        x2 = x1[:, c] + _sigmoid(_dot(hn, wpg_ref[:, c])) * pp[:, c]
        if final:
            pieces.append(x2)
        else:
            out_ref[:, c] = x2
    if final:
        out_ref[...] = _rms(jnp.concatenate(pieces, axis=1), gfin_ref[...])


def _merge(oa, ob, ma, mb, x2d, p3d, wa, wb, wout, gple, wpg, wpp, gfin, *, tm, layer, final):
    n = x2d.shape[0]
    row = lambda w: pl.BlockSpec((tm, w), lambda i: (i, 0))
    full = lambda a: pl.BlockSpec(a.shape, lambda i: (0,) * a.ndim)
    slab = lambda a: pl.BlockSpec((None,) + a.shape[1:], lambda i: (layer, 0, 0),
                                  pipeline_mode=pl.Buffered(1))
    weights = (wa, wb, wout, wpg, wpp)
    return pl.pallas_call(
        functools.partial(_merge_kernel, final=final),
        grid=(n // tm,),
        in_specs=[row(SWA_WIDTH), row(MLA_WIDTH), row(D_MODEL), row(D_MODEL), row(D_MODEL),
                  pl.BlockSpec((None, tm, PLE_DIM), lambda i: (layer, i, 0)),
                  slab(wa), slab(wb), slab(wout), full(gple), slab(wpg), slab(wpp), full(gfin)],
        out_specs=row(D_MODEL),
        out_shape=jax.ShapeDtypeStruct((n, D_MODEL), F32),
        scratch_shapes=[pltpu.VMEM(w.shape[1:], BF16) for w in weights],
        compiler_params=pltpu.CompilerParams(dimension_semantics=("arbitrary",),
                                             vmem_limit_bytes=VMEM_LIMIT),
        name="merge",
    )(oa, ob, ma, mb, x2d, p3d, wa, wb, wout, gple, wpg, wpp, gfin)


def _pack_weights(w_in, w_uq, w_ukv):
    d = w_in.shape[0]
    win = w_in
    wuq = w_uq.reshape(d, MLA_Q_LORA, MLA_HEADS, MLA_QK)
    wuq = jnp.pad(wuq, ((0, 0), (0, 0), (0, 0), (0, LANES - MLA_QK))).reshape(d, MLA_Q_LORA, MLA_PAD)
    wukv = w_ukv.reshape(d, MLA_KV_LORA, MLA_HEADS, MLA_NOPE + MLA_V)
    wuk = wukv[..., :MLA_NOPE].reshape(d, MLA_KV_LORA, MLA_HEADS * MLA_NOPE)
    wuv = wukv[..., MLA_NOPE:].reshape(d, MLA_KV_LORA, MLA_WIDTH)
    return win, wuq.astype(BF16), wuk.astype(BF16), wuv.astype(BF16)


def _rope_inv_tile():
    inv = ROPE_THETA ** (-jnp.arange(0, MLA_ROPE, 2, dtype=F32) / MLA_ROPE)
    return jnp.tile(inv, LANES // (MLA_ROPE // 2))[None, :]


def kernel(x, p, positions, g_mix, w_in, sink, g_q, w_uq, g_kv, w_ukv, w_br_a, w_br_b, w_out,
           g_ple, w_ple_gate, w_ple_proj, g_final):
    b, s, _ = x.shape
    depth = w_in.shape[0]
    n = b * s
    tm_in, tm_merge, swa_tile, tq, tk = 512, 512, 512, 512, 512
    assert n % tm_in == 0 and n % tm_merge == 0 and s % swa_tile == 0 and s % tq == 0
    assert tk == tq
    assert tm_in % 64 == 0 and tm_in // 64 <= LANES // (MLA_ROPE // 2)

    win, wuq, wuk, wuv = _pack_weights(w_in, w_uq, w_ukv)
    inv_tile = _rope_inv_tile()
    pos_in = positions.reshape(n // tm_in, tm_in // LANES, LANES)
    pos4 = positions.reshape(b, s // swa_tile, swa_tile // BLOCK, LANES)
    posh = positions.reshape(b, s // BLOCK, 1, LANES)
    gfin = g_final[None, :]
    p3d = p.reshape(depth, n, PLE_DIM)

    x2d = x.reshape(n, D_MODEL)
    for i in range(depth):
        aq, ak, av, ag, bg, ma, mb, q, k, v = _in_proj(
            x2d, pos_in, inv_tile, g_mix[i][None, :], win, g_q[i][None, :], wuq[i],
            g_kv[i][None, :], wuk[i], wuv[i], tm=tm_in, layer=i)
        r3 = lambda t: t.reshape(b, s, t.shape[-1])
        oa = _swa(sink[i], r3(aq), r3(ak), r3(av), r3(ag), pos4, posh, tile=swa_tile)
        ob = _mla(r3(q), r3(k), r3(v), r3(bg), tq=tq, tk=tk)
        x2d = _merge(oa.reshape(n, SWA_WIDTH), ob.reshape(n, MLA_WIDTH), ma, mb, x2d,
                     p3d, w_br_a, w_br_b, w_out, g_ple[i][None, :], w_ple_gate,
                     w_ple_proj, gfin, tm=tm_merge, layer=i, final=(i == depth - 1))
    return x2d.reshape(b, s, D_MODEL)
```

```python
import functools

import jax
import jax.numpy as jnp
from jax import lax
from jax.experimental import pallas as pl
from jax.experimental.pallas import tpu as pltpu

F32 = jnp.float32
BF16 = jnp.bfloat16

D_MODEL = 1024
PLE_DIM = 256
BLOCK = 128
LANES = 128
EPS = 1e-6
NEG = -1e30
LOG2E = 1.4426950408889634

SWA_HEADS = 8
SWA_KV_HEADS = 2
SWA_HEAD_DIM = 64
SWA_WIDTH = SWA_HEADS * SWA_HEAD_DIM
SWA_KV_WIDTH = SWA_KV_HEADS * SWA_HEAD_DIM

MLA_HEADS = 8
MLA_NOPE = 64
MLA_ROPE = 32
MLA_V = 64
MLA_Q_LORA = 256
MLA_KV_LORA = 128
MLA_WIDTH = MLA_HEADS * MLA_V
MLA_QK = MLA_NOPE + MLA_ROPE
MLA_PAD = MLA_HEADS * LANES
ROPE_THETA = 10000.0

_C_AQ = 0
_C_AK = _C_AQ + SWA_WIDTH
_C_AV = _C_AK + SWA_KV_WIDTH
_C_AG = _C_AV + SWA_KV_WIDTH
_C_QD = _C_AG + SWA_WIDTH
_C_KVD = _C_QD + MLA_Q_LORA
_C_KR = _C_KVD + MLA_KV_LORA
_C_BG = _C_KR + LANES
_C_MA = _C_BG + MLA_WIDTH
_C_MB = _C_MA + D_MODEL
_C_END = _C_MB + D_MODEL

_DIST_MASK = 2.0 ** 8 * 1e30

VMEM_LIMIT = 56 * 1024 * 1024


def _rms(x, g):
    return x * lax.rsqrt(jnp.mean(x * x, axis=-1, keepdims=True) + EPS) * g


def _sigmoid(x):
    return 1.0 / (1.0 + jnp.exp(-x))


def _col_bcast(row):
    return jnp.broadcast_to(row, (LANES, LANES)).T


def _dot(a, b):
    return jnp.dot(a, b, preferred_element_type=F32)


def _dot_nt(a, b):
    return lax.dot_general(a, b, (((1,), (1,)), ((), ())), preferred_element_type=F32)


def _in_proj_kernel(x_ref, pos_ref, inv_ref, gmix_ref, win_ref, gq_ref, wuq_ref, gkv_ref,
                    wuk_ref, wuv_ref,
                    aq_ref, ak_ref, av_ref, ag_ref, bg_ref, ma_ref, mb_ref, q_ref, k_ref, v_ref,
                    wpad_sc, *, tm):
    @pl.when(pl.program_id(0) == 0)
    def _():
        wpad_sc[:, :_C_KR] = win_ref[:, :_C_KR].astype(BF16)
        t = win_ref[:, _C_KR:_C_KR + LANES]
        lane = lax.broadcasted_iota(jnp.int32, t.shape, 1)
        kr_lanes = (lane >= MLA_NOPE) & (lane < MLA_QK)
        wpad_sc[:, _C_KR:_C_BG] = jnp.where(kr_lanes, pltpu.roll(t, MLA_NOPE, 1), 0.0).astype(BF16)
        wpad_sc[:, _C_BG:] = win_ref[:, _C_KR + MLA_ROPE:].astype(BF16)

    h = _rms(x_ref[...], gmix_ref[...]).astype(BF16)

    def proj(lo, hi):
        return _dot(h, wpad_sc[:, lo:hi])

    qd = proj(_C_QD, _C_KVD)
    kvr = proj(_C_KVD, _C_BG)

    half = MLA_ROPE // 2
    grp = 64
    ngrp = tm // grp
    pos = pos_ref[0].astype(F32)
    cols = [_col_bcast(pos[g:g + 1, :]) for g in range(tm // LANES)]
    lane_c = lax.broadcasted_iota(jnp.int32, (grp, LANES), 1)
    posc = jnp.zeros((grp, LANES), F32)
    for g in range(ngrp):
        src = cols[(g * grp) // LANES][(g * grp) % LANES:(g * grp) % LANES + grp, :]
        posc = jnp.where(lane_c // half == g, src, posc)
    ang = posc * inv_ref[...]
    cos_c = jnp.cos(ang)
    sin_c = jnp.sin(ang)
    m_lo = (lane_c >= MLA_NOPE) & (lane_c < MLA_NOPE + half)
    m_hi = (lane_c >= MLA_NOPE + half) & (lane_c < MLA_QK)
    cos_g, slo_g, shi_g = [], [], []
    for g in range(ngrp):
        to_lo = (MLA_NOPE - half * g) % LANES
        to_hi = (MLA_NOPE + half - half * g) % LANES
        c_lo, c_hi = pltpu.roll(cos_c, to_lo, 1), pltpu.roll(cos_c, to_hi, 1)
        s_lo_r, s_hi_r = pltpu.roll(sin_c, to_lo, 1), pltpu.roll(sin_c, to_hi, 1)
        cos_g.append(jnp.where(m_lo, c_lo, jnp.where(m_hi, c_hi, 1.0)))
        slo_g.append(jnp.where(m_lo, -s_lo_r, 0.0))
        shi_g.append(jnp.where(m_hi, s_hi_r, 0.0))
    cos = jnp.concatenate(cos_g, axis=0)
    s_lo = jnp.concatenate(slo_g, axis=0)
    s_hi = jnp.concatenate(shi_g, axis=0)

    lane_t = lax.broadcasted_iota(jnp.int32, (tm, LANES), 1)

    def rope(t, c, s1, s2):
        return (t * c + pltpu.roll(t, LANES - half, 1) * s1 + pltpu.roll(t, half, 1) * s2)

    qscale = MLA_QK ** -0.5 * LOG2E
    wide = 2 * LANES

    def akv_piece():
        akv = proj(_C_AK, _C_AG)
        ak_ref[...] = akv[:, :SWA_KV_WIDTH].astype(BF16)
        av_ref[...] = akv[:, SWA_KV_WIDTH:].astype(BF16)

    def piece(ref, col0, j, scale=None):
        def run():
            t = proj(col0 + j * wide, col0 + (j + 1) * wide)
            ref[:, j * wide:(j + 1) * wide] = (t if scale is None else t * scale).astype(BF16)
        return run

    aq_scale = SWA_HEAD_DIM ** -0.5 * LOG2E
    pieces = [akv_piece] + [piece(aq_ref, _C_AQ, j, aq_scale) for j in range(SWA_WIDTH // wide)]
    for ref, col0, width in ((ag_ref, _C_AG, SWA_WIDTH), (bg_ref, _C_BG, MLA_WIDTH),
                             (ma_ref, _C_MA, D_MODEL), (mb_ref, _C_MB, D_MODEL)):
        pieces += [piece(ref, col0, j) for j in range(width // wide)]

    def some_pieces(n):
        for _ in range(min(n, len(pieces))):
            pieces.pop()()

    some_pieces(4)
    qn = _rms(qd, gq_ref[...]).astype(BF16)
    cq, s1q, s2q = cos * qscale, s_lo * qscale, s_hi * qscale
    for pair in range(MLA_HEADS // 2):
        t = _dot(qn, wuq_ref[:, 2 * pair * LANES:(2 * pair + 2) * LANES])
        for hf in range(2):
            c = slice((2 * pair + hf) * LANES, (2 * pair + hf + 1) * LANES)
            q_ref[:, c] = rope(t[:, hf * LANES:(hf + 1) * LANES], cq, s1q, s2q).astype(BF16)
        some_pieces(1)

    kvn = _rms(kvr[:, :MLA_KV_LORA], gkv_ref[...]).astype(BF16)
    kr = rope(kvr[:, MLA_KV_LORA:], cos, s_lo, s_hi)
    some_pieces(1)
    lo_k = lane_t < MLA_NOPE
    for two_pairs in range(MLA_HEADS // 4):
        t = _dot(kvn, wuk_ref[:, two_pairs * wide:(two_pairs + 1) * wide])
        for j in range(2):
            pair = 2 * two_pairs + j
            pt = t[:, j * LANES:(j + 1) * LANES]
            k_ref[:, 2 * pair * LANES:(2 * pair + 1) * LANES] = (
                jnp.where(lo_k, pt, 0.0) + kr).astype(BF16)
            k_ref[:, (2 * pair + 1) * LANES:(2 * pair + 2) * LANES] = (
                jnp.where(lo_k, pltpu.roll(pt, MLA_NOPE, 1), 0.0) + kr).astype(BF16)
            some_pieces(1)
    lo_v = lane_t < MLA_V
    for two_pairs in range(MLA_HEADS // 4):
        t = _dot(kvn, wuv_ref[:, two_pairs * wide:(two_pairs + 1) * wide])
        for j in range(2):
            pair = 2 * two_pairs + j
            pt = t[:, j * LANES:(j + 1) * LANES]
            v_ref[:, 2 * pair * LANES:(2 * pair + 1) * LANES] = jnp.where(lo_v, pt, 0.0).astype(BF16)
            v_ref[:, (2 * pair + 1) * LANES:(2 * pair + 2) * LANES] = (
                jnp.where(lo_v, 0.0, pt).astype(BF16))
            some_pieces(1)
    some_pieces(len(pieces))


def _in_proj(x2d, pos3, inv_tile, gmix, win, gq, wuq, gkv, wuk, wuv, *, tm, layer):
    n = x2d.shape[0]
    grid = (n // tm,)
    row = lambda w: pl.BlockSpec((tm, w), lambda i: (i, 0))
    full = lambda a: pl.BlockSpec(a.shape, lambda i: (0,) * a.ndim)
    widths = (SWA_WIDTH, SWA_KV_WIDTH, SWA_KV_WIDTH, SWA_WIDTH, MLA_WIDTH, D_MODEL, D_MODEL,
              MLA_PAD, MLA_PAD, MLA_PAD)
    return pl.pallas_call(
        functools.partial(_in_proj_kernel, tm=tm),
        grid=grid,
        in_specs=[row(D_MODEL),
                  pl.BlockSpec((1, tm // LANES, LANES), lambda i: (i, 0, 0)),
                  full(inv_tile), full(gmix),
                  pl.BlockSpec((None,) + win.shape[1:], lambda i: (layer, 0, 0),
                               pipeline_mode=pl.Buffered(1)),
                  full(gq), full(wuq), full(gkv), full(wuk), full(wuv)],
        out_specs=[row(w) for w in widths],
        out_shape=[jax.ShapeDtypeStruct((n, w), BF16) for w in widths],
        scratch_shapes=[pltpu.VMEM((D_MODEL, _C_END), BF16)],
        compiler_params=pltpu.CompilerParams(dimension_semantics=("arbitrary",),
                                             vmem_limit_bytes=VMEM_LIMIT),
        name="in_proj",
    )(x2d, pos3, inv_tile, gmix, win, gq, wuq, gkv, wuk, wuv)


def _swa_kernel(sink_ref, q_ref, k_ref, kh_ref, v_ref, vh_ref, g_ref, pos_ref, posh_ref, o_ref,
                *, tile):
    i = pl.program_id(1)
    nsub = tile // BLOCK
    group = SWA_HEADS // SWA_KV_HEADS

    def variants(main_ref, halo_ref):
        t = jnp.concatenate([halo_ref[0], main_ref[0]], axis=0).astype(F32)
        r = pltpu.roll(t, SWA_HEAD_DIM, 1)
        lo = lax.broadcasted_iota(jnp.int32, t.shape, 1) < SWA_HEAD_DIM
        z = jnp.zeros_like(t)
        return [[jnp.where(lo, t, z).astype(BF16), jnp.where(lo, z, r).astype(BF16)],
                [jnp.where(lo, r, z).astype(BF16), jnp.where(lo, z, t).astype(BF16)]]

    kv = variants(k_ref, kh_ref)
    vv = variants(v_ref, vh_ref)
    pos_rows = jnp.concatenate([posh_ref[0, 0], pos_ref[0, 0]], axis=0)

    ii = lax.broadcasted_iota(jnp.int32, (BLOCK, 2 * BLOCK), 0)
    jj = lax.broadcasted_iota(jnp.int32, (BLOCK, 2 * BLOCK), 1)
    band = (jj > ii) & (jj <= ii + BLOCK)

    for sub in range(nsub):
        rows = slice(sub * BLOCK, (sub + 1) * BLOCK)
        krows = slice(sub * BLOCK, (sub + 2) * BLOCK)
        pq = _col_bcast(pos_rows[sub + 1:sub + 2, :])[:, 0:1]
        pk = jnp.concatenate([pos_rows[sub:sub + 1, :], pos_rows[sub + 1:sub + 2, :]], axis=1)
        dist = (pq - pk).astype(F32)
        valid = band
        if sub == 0:
            valid = band & ((jj >= BLOCK) | (i > 0))
        distm = jnp.where(valid, dist, _DIST_MASK)
        for t in range(SWA_HEADS // 2):
            g = (2 * t) // group
            qp = q_ref[0, rows, t * LANES:(t + 1) * LANES]
            acc = None
            for hf in range(2):
                hd = 2 * t + hf
                slope = 2.0 ** (-8.0 * (hd + 1) / SWA_HEADS) * LOG2E
                sink = sink_ref[hd] * LOG2E
                s = _dot_nt(qp, kv[g][hf][krows]) - slope * distm
                m = jnp.maximum(jnp.max(s, axis=-1, keepdims=True), sink)
                e = jnp.exp2(s - m)
                den = jnp.sum(e, axis=-1, keepdims=True) + jnp.exp2(sink - m)
                pv = _dot(e.astype(BF16), vv[g][hf][krows]) / den
                acc = pv if acc is None else acc + pv
            gate = g_ref[0, rows, t * LANES:(t + 1) * LANES].astype(F32)
            o_ref[0, rows, t * LANES:(t + 1) * LANES] = (acc * gate * _sigmoid(gate)).astype(BF16)


def _swa(sink, q, k, v, gate, pos4, posh, *, tile):
    b, s, _ = q.shape
    nsub = tile // BLOCK
    grid = (b, s // tile)
    main = lambda w: pl.BlockSpec((1, tile, w), lambda bi, i: (bi, i, 0))
    halo = pl.BlockSpec((1, BLOCK, SWA_KV_WIDTH),
                        lambda bi, i: (bi, jnp.maximum(i * nsub - 1, 0), 0))
    return pl.pallas_call(
        functools.partial(_swa_kernel, tile=tile),
        grid=grid,
        in_specs=[pl.BlockSpec(memory_space=pltpu.SMEM),
                  main(SWA_WIDTH), main(SWA_KV_WIDTH), halo, main(SWA_KV_WIDTH), halo,
                  main(SWA_WIDTH),
                  pl.BlockSpec((1, 1, nsub, LANES), lambda bi, i: (bi, i, 0, 0)),
                  pl.BlockSpec((1, 1, 1, LANES),
                               lambda bi, i: (bi, jnp.maximum(i * nsub - 1, 0), 0, 0))],
        out_specs=main(SWA_WIDTH),
        out_shape=jax.ShapeDtypeStruct((b, s, SWA_WIDTH), BF16),
        compiler_params=pltpu.CompilerParams(dimension_semantics=("arbitrary", "arbitrary"),
                                             vmem_limit_bytes=VMEM_LIMIT),
        name="swa",
    )(sink, q, k, k, v, v, gate, pos4, posh)


def _mla_kernel(q_ref, k_ref, v_ref, g_ref, o_ref, m_sc, acc_sc, p_sc, alpha_sc, *, tq, tk):
    def block(qi, carry):
        rows = pl.ds(pl.multiple_of(qi * tq, tq), tq)
        _mla_block(qi, q_ref.at[:, rows, :], k_ref, v_ref, g_ref.at[:, rows, :],
                   o_ref.at[:, rows, :], m_sc, acc_sc, p_sc, alpha_sc, tq=tq, tk=tk)
        return carry

    lax.fori_loop(0, q_ref.shape[1] // tq, block, 0)


def _mla_block(qi, q_ref, k_ref, v_ref, g_ref, o_ref, m_sc, acc_sc, p_sc, alpha_sc, *, tq, tk):
    half = tq // 2
    full = slice(0, tq)
    lo_row = lax.broadcasted_iota(jnp.int32, (1, LANES), 1) < MLA_V

    all_pairs = range(MLA_HEADS // 2)

    def scores(rows, start, width, mask_off, pairs=all_pairs, first=False):
        nr = rows.stop - rows.start
        lo = jnp.broadcast_to(lo_row, (nr, LANES))
        if mask_off is not None:
            row = lax.broadcasted_iota(jnp.int32, (nr, width), 0)
            col = lax.broadcasted_iota(jnp.int32, (nr, width), 1)
            visible = col <= row + mask_off
        for pair in pairs:
            alphas = []
            for hd in (2 * pair, 2 * pair + 1):
                c = slice(hd * LANES, (hd + 1) * LANES)
                s = _dot_nt(q_ref[0, rows, c], k_ref[0, pl.ds(start, width), c])
                if mask_off is not None:
                    s = jnp.where(visible, s, NEG)
                m_cur = jnp.max(s, axis=-1, keepdims=True)
                if first:
                    m_new = jnp.broadcast_to(m_cur, (nr, LANES))
                else:
                    m_prev = m_sc[hd, rows]
                    m_new = jnp.maximum(m_prev, m_cur)
                    alphas.append(jnp.exp2(m_prev - m_new))
                p_sc[hd, rows, 0:width] = jnp.exp2(
                    s - jnp.tile(m_new, (1, width // LANES))).astype(BF16)
                m_sc[hd, rows] = m_new
            if first:
                alpha_sc[pair, rows] = jnp.zeros((nr, LANES), F32)
            else:
                alpha_sc[pair, rows] = jnp.where(lo, alphas[0], alphas[1])

    def weighted(rows, start, width, pairs=all_pairs, first=False):
        ones = [jnp.broadcast_to(jnp.where(lo_row, 1.0, 0.0).astype(BF16), (width, LANES)),
                jnp.broadcast_to(jnp.where(lo_row, 0.0, 1.0).astype(BF16), (width, LANES))]
        for pair in pairs:
            ps, vts = [], []
            for hf in range(2):
                hd = 2 * pair + hf
                c = slice(hd * LANES, (hd + 1) * LANES)
                ps.append(p_sc[hd, rows, 0:width])
                vts.append(jnp.concatenate([v_ref[0, pl.ds(start, width), c], ones[hf]], axis=1))
            pv = _dot(jnp.concatenate(ps, axis=1), jnp.concatenate(vts, axis=0))
            if first:
                acc_sc[pair, rows] = pv
            else:
                acc_sc[pair, rows] = (jnp.tile(alpha_sc[pair, rows], (1, 2)) * acc_sc[pair, rows]
                                      + pv)

    top, bottom = slice(0, half), slice(half, tq)
    diag = pl.multiple_of(qi * tq, tq)

    @pl.when(qi == 0)
    def _():
        scores(top, 0, half, 0, first=True)
        scores(bottom, 0, tq, half, first=True)
        weighted(top, 0, half, first=True)
        weighted(bottom, 0, tq, first=True)

    @pl.when(qi > 0)
    def _():
        acc_sc[...] = jnp.zeros(acc_sc.shape, F32)
        scores(full, 0, tk, None, first=True)

        def body(j, c):
            for pair in all_pairs:
                weighted(full, pl.multiple_of((j - 1) * tk, tk), tk, (pair,))
                scores(full, pl.multiple_of(j * tk, tk), tk, None, (pair,))
            return c

        lax.fori_loop(1, qi, body, 0)
        prev = pl.multiple_of((qi - 1) * tk, tk)
        for pair in all_pairs:
            weighted(full, prev, tk, (pair,))
            scores(top, diag, half, 0, (pair,))
            scores(bottom, diag, tq, half, (pair,))
        weighted(top, diag, half)
        weighted(bottom, diag, tq)

    for pair in range(MLA_HEADS // 2):
        acc = acc_sc[pair]
        vcols = slice(pair * LANES, (pair + 1) * LANES)
        gate = g_ref[0, :, vcols].astype(F32)
        o_ref[0, :, vcols] = (acc[:, :LANES] / acc[:, LANES:] * gate * _sigmoid(gate)).astype(BF16)


def _mla(q, k, v, gate, *, tq, tk):
    b, s, _ = q.shape
    seq = lambda w: pl.BlockSpec((1, s, w), lambda bi: (bi, 0, 0))
    return pl.pallas_call(
        functools.partial(_mla_kernel, tq=tq, tk=tk),
        grid=(b,),
        in_specs=[seq(MLA_PAD), seq(MLA_PAD), seq(MLA_PAD), seq(MLA_WIDTH)],
        out_specs=seq(MLA_WIDTH),
        out_shape=jax.ShapeDtypeStruct((b, s, MLA_WIDTH), BF16),
        scratch_shapes=[pltpu.VMEM((MLA_HEADS, tq, LANES), F32),
                        pltpu.VMEM((MLA_HEADS // 2, tq, 2 * LANES), F32),
                        pltpu.VMEM((MLA_HEADS, tq, tk), BF16),
                        pltpu.VMEM((MLA_HEADS // 2, tq, LANES), F32)],
        compiler_params=pltpu.CompilerParams(dimension_semantics=("arbitrary",),
                                             vmem_limit_bytes=VMEM_LIMIT),
        name="mla",
    )(q, k, v, gate)


def _merge_kernel(oa_ref, ob_ref, ma_ref, mb_ref, x_ref, p_ref, wa32_ref, wb32_ref, wout32_ref,
                  gple_ref, wpg32_ref, wpp32_ref, gfin_ref, out_ref,
                  wa_ref, wb_ref, wout_ref, wpg_ref, wpp_ref, *, final):
    @pl.when(pl.program_id(0) == 0)
    def _():
        for src, dst in ((wa32_ref, wa_ref), (wb32_ref, wb_ref), (wout32_ref, wout_ref),
                         (wpg32_ref, wpg_ref), (wpp32_ref, wpp_ref)):
            dst[...] = src[...].astype(BF16)

    ya = _dot(oa_ref[...], wa_ref[...])
    yb = _dot(ob_ref[...], wb_ref[...])
    y = _sigmoid(ma_ref[...].astype(F32)) * ya + _sigmoid(mb_ref[...].astype(F32)) * yb
    x1 = x_ref[...] + _dot(y.astype(BF16), wout_ref[...])
    pp = _dot(p_ref[...].astype(BF16), wpp_ref[...])
    hn = _rms(x1, gple_ref[...]).astype(BF16)
    wide = 2 * LANES
    pieces = []
    for j in range(D_MODEL // wide):
        c = slice(j * wide, (j + 1) * wide)
        x2 = x1[:, c] + _sigmoid(_dot(hn, wpg_ref[:, c])) * pp[:, c]
        if final:
            pieces.append(x2)
        else:
            out_ref[:, c] = x2
    if final:
        out_ref[...] = _rms(jnp.concatenate(pieces, axis=1), gfin_ref[...])


def _merge(oa, ob, ma, mb, x2d, p3d, wa, wb, wout, gple, wpg, wpp, gfin, *, tm, layer, final):
    n = x2d.shape[0]
    row = lambda w: pl.BlockSpec((tm, w), lambda i: (i, 0))
    full = lambda a: pl.BlockSpec(a.shape, lambda i: (0,) * a.ndim)
    slab = lambda a: pl.BlockSpec((None,) + a.shape[1:], lambda i: (layer, 0, 0),
                                  pipeline_mode=pl.Buffered(1))
    weights = (wa, wb, wout, wpg, wpp)
    return pl.pallas_call(
        functools.partial(_merge_kernel, final=final),
        grid=(n // tm,),
        in_specs=[row(SWA_WIDTH), row(MLA_WIDTH), row(D_MODEL), row(D_MODEL), row(D_MODEL),
                  pl.BlockSpec((None, tm, PLE_DIM), lambda i: (layer, i, 0)),
                  slab(wa), slab(wb), slab(wout), full(gple), slab(wpg), slab(wpp), full(gfin)],
        out_specs=row(D_MODEL),
        out_shape=jax.ShapeDtypeStruct((n, D_MODEL), F32),
        scratch_shapes=[pltpu.VMEM(w.shape[1:], BF16) for w in weights],
        compiler_params=pltpu.CompilerParams(dimension_semantics=("arbitrary",),
                                             vmem_limit_bytes=VMEM_LIMIT),
        name="merge",
    )(oa, ob, ma, mb, x2d, p3d, wa, wb, wout, gple, wpg, wpp, gfin)


def _pack_weights(w_in, w_uq, w_ukv):
    d = w_in.shape[0]
    win = w_in
    wuq = w_uq.reshape(d, MLA_Q_LORA, MLA_HEADS, MLA_QK)
    wuq = jnp.pad(wuq, ((0, 0), (0, 0), (0, 0), (0, LANES - MLA_QK))).reshape(d, MLA_Q_LORA, MLA_PAD)
    wukv = w_ukv.reshape(d, MLA_KV_LORA, MLA_HEADS, MLA_NOPE + MLA_V)
    wuk = wukv[..., :MLA_NOPE].reshape(d, MLA_KV_LORA, MLA_HEADS * MLA_NOPE)
    wuv = wukv[..., MLA_NOPE:].reshape(d, MLA_KV_LORA, MLA_WIDTH)
    return win, wuq.astype(BF16), wuk.astype(BF16), wuv.astype(BF16)


def _rope_inv_tile():
    inv = ROPE_THETA ** (-jnp.arange(0, MLA_ROPE, 2, dtype=F32) / MLA_ROPE)
    return jnp.tile(inv, LANES // (MLA_ROPE // 2))[None, :]


def kernel(x, p, positions, g_mix, w_in, sink, g_q, w_uq, g_kv, w_ukv, w_br_a, w_br_b, w_out,
           g_ple, w_ple_gate, w_ple_proj, g_final):
    b, s, _ = x.shape
    depth = w_in.shape[0]
    n = b * s
    tm_in, tm_merge, swa_tile, tq, tk = 512, 512, 512, 512, 512
    assert n % tm_in == 0 and n % tm_merge == 0 and s % swa_tile == 0 and s % tq == 0
    assert tk == tq
    assert tm_in % 64 == 0 and tm_in // 64 <= LANES // (MLA_ROPE // 2)

    win, wuq, wuk, wuv = _pack_weights(w_in, w_uq, w_ukv)
    inv_tile = _rope_inv_tile()
    pos_in = positions.reshape(n // tm_in, tm_in // LANES, LANES)
    pos4 = positions.reshape(b, s // swa_tile, swa_tile // BLOCK, LANES)
    posh = positions.reshape(b, s // BLOCK, 1, LANES)
    gfin = g_final[None, :]
    p3d = p.reshape(depth, n, PLE_DIM)

    x2d = x.reshape(n, D_MODEL)
    for i in range(depth):
        aq, ak, av, ag, bg, ma, mb, q, k, v = _in_proj(
            x2d, pos_in, inv_tile, g_mix[i][None, :], win, g_q[i][None, :], wuq[i],
            g_kv[i][None, :], wuk[i], wuv[i], tm=tm_in, layer=i)
        r3 = lambda t: t.reshape(b, s, t.shape[-1])
        oa = _swa(sink[i], r3(aq), r3(ak), r3(av), r3(ag), pos4, posh, tile=swa_tile)
        ob = _mla(r3(q), r3(k), r3(v), r3(bg), tq=tq, tk=tk)
        x2d = _merge(oa.reshape(n, SWA_WIDTH), ob.reshape(n, MLA_WIDTH), ma, mb, x2d,
                     p3d, w_br_a, w_br_b, w_out, g_ple[i][None, :], w_ple_gate,
                     w_ple_proj, gfin, tm=tm_merge, layer=i, final=(i == depth - 1))
    return x2d.reshape(b, s, D_MODEL)
```
